```python
import jax, jax.numpy as jnp
from jax import lax
import numpy as np

D_MODEL = 2048
BATCH = 1
SEQ = 16384
DEPTH = 1
DEC_BATCH = 4
DEC_SEQ = 8192
PAST_LEN = 128

EPS = 1e-6
D_FF = 5632
CHUNK = 128
A_HEADS = 8
A_WIDTH = D_MODEL // 2
A_HEAD_DIM = A_WIDTH // A_HEADS
POOL_WINDOWS = (2, 4, 8, 16)
POOL_GROUPS = len(POOL_WINDOWS)
B_WIDTH = D_MODEL // 2
POOL_GROUP_DIM = B_WIDTH // POOL_GROUPS
IN_WIDTH = 2 * A_WIDTH + B_WIDTH + 2 * D_MODEL
SPLITS = (A_WIDTH, 2 * A_WIDTH, 2 * A_WIDTH + B_WIDTH, 2 * A_WIDTH + B_WIDTH + D_MODEL)

kernel_name = "hybrid_gmlp_pool_macaron_encoder"


def rms_norm(x, gain):
    xf = x.astype(jnp.float32)
    y = xf * lax.rsqrt(jnp.mean(xf * xf, axis=-1, keepdims=True) + EPS)
    return (y * gain.astype(jnp.float32)).astype(x.dtype)


def swiglu_ffn(x, w1, w3, w2):
    return (jax.nn.silu(x @ w1) * (x @ w3)) @ w2


def spatial_gating_unit(u, v, v_gain, w_spatial, b_spatial):
    b, s, _ = u.shape
    vh = rms_norm(v, v_gain).reshape(b, s // CHUNK, CHUNK, A_HEADS, A_HEAD_DIM)
    mixed = jnp.einsum('hqk,bnkhc->bnqhc', w_spatial, vh) + b_spatial.T[:, :, None]
    return u * mixed.reshape(b, s, A_WIDTH)


def multiscale_pool(z, w_pool, pool_scale):
    b, s, _ = z.shape
    zg = z.astype(jnp.float32).reshape(b, s, POOL_GROUPS, POOL_GROUP_DIM)
    cs = jnp.concatenate([jnp.zeros_like(zg[:, :1]), jnp.cumsum(zg, axis=1)], axis=1)
    t = jnp.arange(s)
    outs = []
    for g, w in enumerate(POOL_WINDOWS):
        half = w // 2
        csg = jnp.pad(cs[:, :, g], ((0, 0), (half, half), (0, 0)), mode='edge')
        win_sum = csg[:, 2 * half:2 * half + s] - csg[:, :s]
        count = (jnp.minimum(t + half, s) - jnp.maximum(t - half, 0)).astype(jnp.float32)
        outs.append(win_sum / count[None, :, None] - zg[:, :, g])
    pooled = jnp.stack(outs, axis=2).astype(z.dtype)
    mixed = jnp.einsum('bsgc,gcd->bsgd', pooled, w_pool).reshape(b, s, B_WIDTH)
    return mixed * pool_scale


def encoder_layer(x, ffn1_norm, ffn1_w1, ffn1_w3, ffn1_w2, mix_norm, w_in, v_norm,
                  w_spatial, b_spatial, w_proj_a, w_pool, pool_scale, w_proj_b, w_out,
                  ffn2_norm, ffn2_w1, ffn2_w3, ffn2_w2):
    h = x + 0.5 * swiglu_ffn(rms_norm(x, ffn1_norm), ffn1_w1, ffn1_w3, ffn1_w2)
    n = rms_norm(h, mix_norm)
    u, v, z, ga, gb = jnp.split(n @ w_in, SPLITS, axis=-1)
    y_a = spatial_gating_unit(u, v, v_norm, w_spatial, b_spatial) @ w_proj_a
    y_b = multiscale_pool(z, w_pool, pool_scale) @ w_proj_b
    merged = jax.nn.sigmoid(ga) * y_a + jax.nn.sigmoid(gb) * y_b
    h = h + merged @ w_out
    h = h + 0.5 * swiglu_ffn(rms_norm(h, ffn2_norm), ffn2_w1, ffn2_w3, ffn2_w2)
    return h


def setup_inputs(seed: int = 0) -> dict:
    key = jax.random.key(seed)
    ks = iter(jax.random.split(key, 32))
    f32 = jnp.float32

    def nrm(shape, scale):
        return jax.random.normal(next(ks), shape, f32) * scale

    def gain(shape):
        return 1.0 + 0.02 * jax.random.normal(next(ks), shape, f32)

    L, D = DEPTH, D_MODEL
    return {
        "x_prompt": jax.random.normal(next(ks), (BATCH, SEQ, D), f32),
        "x_sample": jax.random.normal(next(ks), (DEC_BATCH, DEC_SEQ, D), f32),
        "ffn1_norm": gain((L, D)),
        "ffn1_w1": nrm((L, D, D_FF), D ** -0.5),
        "ffn1_w3": nrm((L, D, D_FF), D ** -0.5),
        "ffn1_w2": nrm((L, D_FF, D), D_FF ** -0.5),
        "mix_norm": gain((L, D)),
        "w_in": nrm((L, D, IN_WIDTH), D ** -0.5),
        "v_norm": gain((L, A_WIDTH)),
        "w_spatial": nrm((L, A_HEADS, CHUNK, CHUNK), CHUNK ** -0.5),
        "b_spatial": nrm((L, A_HEADS, CHUNK), 0.02),
        "w_proj_a": nrm((L, A_WIDTH, D), A_WIDTH ** -0.5),
        "w_pool": nrm((L, POOL_GROUPS, POOL_GROUP_DIM, POOL_GROUP_DIM), POOL_GROUP_DIM ** -0.5),
        "pool_scale": gain((L, B_WIDTH)),
        "w_proj_b": nrm((L, B_WIDTH, D), B_WIDTH ** -0.5),
        "w_out": nrm((L, D, D), D ** -0.5),
        "ffn2_norm": gain((L, D)),
        "ffn2_w1": nrm((L, D, D_FF), D ** -0.5),
        "ffn2_w3": nrm((L, D, D_FF), D ** -0.5),
        "ffn2_w2": nrm((L, D_FF, D), D_FF ** -0.5),
        "final_norm": gain((D,)),
    }


def reference(x_prompt, x_sample, ffn1_norm, ffn1_w1, ffn1_w3, ffn1_w2, mix_norm, w_in,
              v_norm, w_spatial, b_spatial, w_proj_a, w_pool, pool_scale, w_proj_b, w_out,
              ffn2_norm, ffn2_w1, ffn2_w3, ffn2_w2, final_norm):
    hp = x_prompt
    hs = x_sample
    for l in range(DEPTH):
        p = (ffn1_norm[l], ffn1_w1[l], ffn1_w3[l], ffn1_w2[l], mix_norm[l], w_in[l], v_norm[l],
             w_spatial[l], b_spatial[l], w_proj_a[l], w_pool[l], pool_scale[l], w_proj_b[l],
             w_out[l], ffn2_norm[l], ffn2_w1[l], ffn2_w3[l], ffn2_w2[l])
        hp = encoder_layer(hp, *p)
        hs = encoder_layer(hs, *p)
    y_prompt = rms_norm(hp, final_norm)
    y_sample = rms_norm(hs, final_norm)
    return (y_prompt, y_sample)
```

```python
import functools

import jax
import jax.numpy as jnp
import numpy as np
from jax import lax
from jax.experimental import pallas as pl
from jax.experimental.pallas import tpu as pltpu

EPS = 1e-6
D_MODEL = 2048
D_FF = 5632
CHUNK = 128
A_HEADS = 8
A_WIDTH = D_MODEL // 2
A_HEAD_DIM = A_WIDTH // A_HEADS
POOL_WINDOWS = (2, 4, 8, 16)
POOL_GROUPS = len(POOL_WINDOWS)
B_WIDTH = D_MODEL // 2
POOL_GROUP_DIM = B_WIDTH // POOL_GROUPS
IN_WIDTH = 2 * A_WIDTH + B_WIDTH + 2 * D_MODEL
Z_OFF = 2 * A_WIDTH
GA_OFF = Z_OFF + B_WIDTH
GB_OFF = GA_OFF + D_MODEL

BF16_SUBLANES = 16
HALO = BF16_SUBLANES
VMEM_LIMIT_BYTES = 56 * 1024 * 1024

FFN_TM = 512
FFN_TF = 512
PROJ_TM = 1024
PROJ_TN = 1024
MIX_TM = 256

_F32 = jnp.float32
_BF16 = jnp.bfloat16


def _rms_norm(x, gain):
    return x * lax.rsqrt(jnp.mean(x * x, axis=-1, keepdims=True) + EPS) * gain


def _dot(a, b):
    return jnp.dot(a, b, preferred_element_type=_F32)


def _ffn_kernel(x_ref, g_ref, w1_ref, w3_ref, w2_ref, go_ref, o_ref, n_scr, *, final_norm):
    f = pl.program_id(1)

    @pl.when(f == 0)
    def _():
        n_scr[...] = _rms_norm(x_ref[...], g_ref[...]).astype(_BF16)
        o_ref[...] = jnp.zeros_like(o_ref)

    n = n_scr[...]
    h1 = _dot(n, w1_ref[...])
    h3 = _dot(n, w3_ref[...])
    gated = (h1 * jax.nn.sigmoid(h1) * h3).astype(_BF16)
    o_ref[...] += _dot(gated, w2_ref[...])

    @pl.when(f == pl.num_programs(1) - 1)
    def _():
        h = x_ref[...] + 0.5 * o_ref[...]
        if final_norm:
            h = _rms_norm(h, go_ref[...])
        o_ref[...] = h


def _ffn(x, gain, w1, w3, w2, out_gain, *, final_norm):
    t = x.shape[0]
    tm, tf = FFN_TM, FFN_TF
    return pl.pallas_call(
        functools.partial(_ffn_kernel, final_norm=final_norm),
        grid=(t // tm, D_FF // tf),
        in_specs=[
            pl.BlockSpec((tm, D_MODEL), lambda i, f: (i, 0)),
            pl.BlockSpec((1, D_MODEL), lambda i, f: (0, 0)),
            pl.BlockSpec((D_MODEL, tf), lambda i, f: (0, f)),
            pl.BlockSpec((D_MODEL, tf), lambda i, f: (0, f)),
            pl.BlockSpec((tf, D_MODEL), lambda i, f: (f, 0)),
            pl.BlockSpec((1, D_MODEL), lambda i, f: (0, 0)),
        ],
        out_specs=pl.BlockSpec((tm, D_MODEL), lambda i, f: (i, 0)),
        out_shape=jax.ShapeDtypeStruct((t, D_MODEL), _F32),
        scratch_shapes=[pltpu.VMEM((tm, D_MODEL), _BF16)],
        compiler_params=pltpu.CompilerParams(
            dimension_semantics=("arbitrary", "arbitrary"),
            vmem_limit_bytes=VMEM_LIMIT_BYTES),
        name="ffn_final" if final_norm else "ffn",
    )(x, gain, w1, w3, w2, out_gain)


def _proj_kernel(h_ref, g_ref, w_ref, vg_ref, o_ref, n_scr):
    j = pl.program_id(1)

    @pl.when(j == 0)
    def _():
        n_scr[...] = _rms_norm(h_ref[...], g_ref[...]).astype(_BF16)

    p = _dot(n_scr[...], w_ref[...])

    @pl.when(j == A_WIDTH // PROJ_TN)
    def _():
        o_ref[...] = _rms_norm(p, vg_ref[...]).astype(_BF16)

    @pl.when(j != A_WIDTH // PROJ_TN)
    def _():
        o_ref[...] = p.astype(_BF16)


def _proj(h, gain, w_in, v_gain):
    t = h.shape[0]
    tm, tn = PROJ_TM, PROJ_TN
    return pl.pallas_call(
        _proj_kernel,
        grid=(t // tm, IN_WIDTH // tn),
        in_specs=[
            pl.BlockSpec((tm, D_MODEL), lambda i, j: (i, 0)),
            pl.BlockSpec((1, D_MODEL), lambda i, j: (0, 0)),
            pl.BlockSpec((D_MODEL, tn), lambda i, j: (0, j)),
            pl.BlockSpec((1, A_WIDTH), lambda i, j: (0, 0)),
        ],
        out_specs=pl.BlockSpec((tm, tn), lambda i, j: (i, j)),
        out_shape=jax.ShapeDtypeStruct((t, IN_WIDTH), _BF16),
        scratch_shapes=[pltpu.VMEM((tm, D_MODEL), _BF16)],
        compiler_params=pltpu.CompilerParams(
            dimension_semantics=("arbitrary", "arbitrary"),
            vmem_limit_bytes=VMEM_LIMIT_BYTES),
        name="proj",
    )(h, gain, w_in, v_gain)


def _pool_band_matrices(tm):
    t = np.arange(tm)[:, None]
    j = np.arange(tm + 2 * HALO)[None, :] - HALO
    mats = [((j - t >= -(w // 2)) & (j - t <= w // 2 - 1)) for w in POOL_WINDOWS]
    return jnp.asarray(np.stack(mats).astype(np.float32), dtype=_BF16)


def _mixer_kernel(proj_ref, zprev_ref, znext_ref, h_ref, ws_ref, bs_ref, wpa_ref, band_ref,
                  wpool_ref, pscale_ref, wpb_ref, wout_ref, o_ref, a_scr, b_scr, *, tm, seq_len):
    tiles_per_seq = seq_len // tm
    tile_in_seq = pl.program_id(0) % tiles_per_seq

    for c in range(tm // CHUNK):
        rows = slice(c * CHUNK, (c + 1) * CHUNK)
        for hd in range(A_HEADS):
            cols = slice(hd * A_HEAD_DIM, (hd + 1) * A_HEAD_DIM)
            vcols = slice(A_WIDTH + hd * A_HEAD_DIM, A_WIDTH + (hd + 1) * A_HEAD_DIM)
            mixed = _dot(ws_ref[hd], proj_ref[rows, vcols]) + bs_ref[hd]
            a_scr[rows, cols] = (proj_ref[rows, cols].astype(_F32) * mixed).astype(_BF16)
    y_a = _dot(a_scr[...], wpa_ref[...])

    z_tile = proj_ref[:, Z_OFF:Z_OFF + B_WIDTH]
    z_prev = jnp.where(tile_in_seq == 0, jnp.zeros_like(zprev_ref), zprev_ref[...])
    z_next = jnp.where(tile_in_seq == tiles_per_seq - 1, jnp.zeros_like(znext_ref), znext_ref[...])
    z_ext = jnp.concatenate([z_prev, z_tile, z_next], axis=0)
    t = tile_in_seq * tm + lax.broadcasted_iota(jnp.int32, (tm, 1), 0)
    for g, w in enumerate(POOL_WINDOWS):
        gcols = slice(g * POOL_GROUP_DIM, (g + 1) * POOL_GROUP_DIM)
        half = w // 2
        count = (jnp.minimum(t + half, seq_len) - jnp.maximum(t - half, 0)).astype(_F32)
        win_sum = _dot(band_ref[g], z_ext[:, gcols])
        pooled = win_sum / count - z_tile[:, gcols].astype(_F32)
        mixed_b = _dot(pooled.astype(_BF16), wpool_ref[g]) * pscale_ref[:, gcols]
        b_scr[:, gcols] = mixed_b.astype(_BF16)
    y_b = _dot(b_scr[...], wpb_ref[...])

    ga = proj_ref[:, GA_OFF:GA_OFF + D_MODEL].astype(_F32)
    gb = proj_ref[:, GB_OFF:GB_OFF + D_MODEL].astype(_F32)
    merged = jax.nn.sigmoid(ga) * y_a + jax.nn.sigmoid(gb) * y_b
    o_ref[...] = h_ref[...] + _dot(merged.astype(_BF16), wout_ref[...])


def _mixer(proj, h, w_spatial, b_spatial, w_proj_a, w_pool, pool_scale, w_proj_b, w_out, *, seq_len):
    t = h.shape[0]
    tm = MIX_TM
    halo_blocks_per_tile = tm // HALO
    last_halo_block = t // HALO - 1
    z_block = Z_OFF // B_WIDTH

    def const(shape):
        return pl.BlockSpec(shape, lambda i: (0,) * len(shape), pipeline_mode=pl.Buffered(1))

    return pl.pallas_call(
        functools.partial(_mixer_kernel, tm=tm, seq_len=seq_len),
        grid=(t // tm,),
        in_specs=[
            pl.BlockSpec((tm, IN_WIDTH), lambda i: (i, 0)),
            pl.BlockSpec((HALO, B_WIDTH),
                         lambda i: (jnp.maximum(i * halo_blocks_per_tile - 1, 0), z_block)),
            pl.BlockSpec((HALO, B_WIDTH),
                         lambda i: (jnp.minimum((i + 1) * halo_blocks_per_tile, last_halo_block), z_block)),
            pl.BlockSpec((tm, D_MODEL), lambda i: (i, 0)),
            const((A_HEADS, CHUNK, CHUNK)),
            const((A_HEADS, CHUNK, 1)),
            const((A_WIDTH, D_MODEL)),
            const((POOL_GROUPS, tm, tm + 2 * HALO)),
            const((POOL_GROUPS, POOL_GROUP_DIM, POOL_GROUP_DIM)),
            const((1, B_WIDTH)),
            const((B_WIDTH, D_MODEL)),
            const((D_MODEL, D_MODEL)),
        ],
        out_specs=pl.BlockSpec((tm, D_MODEL), lambda i: (i, 0)),
        out_shape=jax.ShapeDtypeStruct((t, D_MODEL), _F32),
        scratch_shapes=[pltpu.VMEM((tm, A_WIDTH), _BF16), pltpu.VMEM((tm, B_WIDTH), _BF16)],
        compiler_params=pltpu.CompilerParams(
            dimension_semantics=("arbitrary",),
            vmem_limit_bytes=VMEM_LIMIT_BYTES),
        name="mixer",
    )(proj, proj, proj, h, w_spatial, b_spatial, w_proj_a, _pool_band_matrices(tm),
      w_pool, pool_scale, w_proj_b, w_out)


def kernel(x_prompt, x_sample, ffn1_norm, ffn1_w1, ffn1_w3, ffn1_w2, mix_norm, w_in, v_norm,
           w_spatial, b_spatial, w_proj_a, w_pool, pool_scale, w_proj_b, w_out,
           ffn2_norm, ffn2_w1, ffn2_w3, ffn2_w2, final_norm):
    depth = ffn1_w1.shape[0]
    assert depth >= 1, "the output norm is fused into the last layer's second FFN"
    bf = lambda w: w.astype(_BF16)
    rows = lambda v: v.reshape(v.shape[0], 1, -1)
    final_gain = final_norm.reshape(1, -1)
    ffn1 = (rows(ffn1_norm), bf(ffn1_w1), bf(ffn1_w3), bf(ffn1_w2))
    ffn2 = (rows(ffn2_norm), bf(ffn2_w1), bf(ffn2_w3), bf(ffn2_w2))
    proj_p = (rows(mix_norm), bf(w_in), rows(v_norm))
    mix_p = (bf(w_spatial), b_spatial[..., None], bf(w_proj_a), bf(w_pool), rows(pool_scale),
             bf(w_proj_b), bf(w_out))

    outs = []
    for x in (x_prompt, x_sample):
        batch, seq_len, _ = x.shape
        h = x.reshape(batch * seq_len, D_MODEL)
        for l in range(depth):
            h = _ffn(h, *(p[l] for p in ffn1), final_gain, final_norm=False)
            proj = _proj(h, *(p[l] for p in proj_p))
            h = _mixer(proj, h, *(p[l] for p in mix_p), seq_len=seq_len)
            h = _ffn(h, *(p[l] for p in ffn2), final_gain, final_norm=(l == depth - 1))
        outs.append(h.reshape(batch, seq_len, D_MODEL))
    return tuple(outs)
```

```python
import functools

import jax
import jax.numpy as jnp
import numpy as np
from jax import lax
from jax.experimental import pallas as pl
from jax.experimental.pallas import tpu as pltpu

EPS = 1e-6
D_MODEL = 2048
D_FF = 5632
CHUNK = 128
A_HEADS = 8
A_WIDTH = D_MODEL // 2
A_HEAD_DIM = A_WIDTH // A_HEADS
POOL_WINDOWS = (2, 4, 8, 16)
POOL_GROUPS = len(POOL_WINDOWS)
B_WIDTH = D_MODEL // 2
POOL_GROUP_DIM = B_WIDTH // POOL_GROUPS
IN_WIDTH = 2 * A_WIDTH + B_WIDTH + 2 * D_MODEL
Z_OFF = 2 * A_WIDTH
GA_OFF = Z_OFF + B_WIDTH
GB_OFF = GA_OFF + D_MODEL

BF16_SUBLANES = 16
HALO = BF16_SUBLANES
VMEM_LIMIT_BYTES = 60 * 1024 * 1024

FFN_TM = 1024
FFN_TF = 512
PROJ_TM = 1024
PROJ_TN = 1792
MIX_TM = 256

_F32 = jnp.float32
_BF16 = jnp.bfloat16


def _rms_norm(x, gain):
    return x * lax.rsqrt(jnp.mean(x * x, axis=-1, keepdims=True) + EPS) * gain


def _dot(a, b):
    return jnp.dot(a, b, preferred_element_type=_F32)


def _ffn_kernel(x_ref, g_ref, w1_ref, w3_ref, w2_ref, go_ref, o_ref, n_scr, *, final_norm):
    f = pl.program_id(1)

    @pl.when(f == 0)
    def _():
        x = x_ref[...]
        n_scr[...] = _rms_norm(x, g_ref[...]).astype(_BF16)
        o_ref[...] = x

    n = n_scr[...]
    h1 = _dot(n, w1_ref[...])
    h3 = _dot(n, w3_ref[...])
    gated = (h1 * jax.nn.sigmoid(h1) * (0.5 * h3)).astype(_BF16)
    o_ref[...] += _dot(gated, w2_ref[...])

    if final_norm:
        @pl.when(f == pl.num_programs(1) - 1)
        def _():
            o_ref[...] = _rms_norm(o_ref[...], go_ref[...])


def _ffn(x, gain, w1, w3, w2, out_gain, *, final_norm):
    t = x.shape[0]
    tm, tf = FFN_TM, FFN_TF
    return pl.pallas_call(
        functools.partial(_ffn_kernel, final_norm=final_norm),
        grid=(t // tm, D_FF // tf),
        in_specs=[
            pl.BlockSpec((tm, D_MODEL), lambda i, f: (i, 0)),
            pl.BlockSpec((1, D_MODEL), lambda i, f: (0, 0)),
            pl.BlockSpec((D_MODEL, tf), lambda i, f: (0, f)),
            pl.BlockSpec((D_MODEL, tf), lambda i, f: (0, f)),
            pl.BlockSpec((tf, D_MODEL), lambda i, f: (f, 0)),
            pl.BlockSpec((1, D_MODEL), lambda i, f: (0, 0)),
        ],
        out_specs=pl.BlockSpec((tm, D_MODEL), lambda i, f: (i, 0)),
        out_shape=jax.ShapeDtypeStruct((t, D_MODEL), _F32),
        scratch_shapes=[pltpu.VMEM((tm, D_MODEL), _BF16)],
        compiler_params=pltpu.CompilerParams(
            dimension_semantics=("arbitrary", "arbitrary"),
            vmem_limit_bytes=VMEM_LIMIT_BYTES),
        name="ffn_final" if final_norm else "ffn",
    )(x, gain, w1, w3, w2, out_gain)


def _proj_kernel(h_ref, g_ref, w_ref, o_ref, n_scr):
    @pl.when(pl.program_id(1) == 0)
    def _():
        n_scr[...] = _rms_norm(h_ref[...], g_ref[...]).astype(_BF16)

    o_ref[...] = _dot(n_scr[...], w_ref[...]).astype(_BF16)


def _proj(h, gain, w_in):
    t = h.shape[0]
    tm, tn = PROJ_TM, PROJ_TN
    return pl.pallas_call(
        _proj_kernel,
        grid=(t // tm, IN_WIDTH // tn),
        in_specs=[
            pl.BlockSpec((tm, D_MODEL), lambda i, j: (i, 0)),
            pl.BlockSpec((1, D_MODEL), lambda i, j: (0, 0)),
            pl.BlockSpec((D_MODEL, tn), lambda i, j: (0, j)),
        ],
        out_specs=pl.BlockSpec((tm, tn), lambda i, j: (i, j)),
        out_shape=jax.ShapeDtypeStruct((t, IN_WIDTH), _BF16),
        scratch_shapes=[pltpu.VMEM((tm, D_MODEL), _BF16)],
        compiler_params=pltpu.CompilerParams(
            dimension_semantics=("arbitrary", "arbitrary"),
            vmem_limit_bytes=VMEM_LIMIT_BYTES),
        name="proj",
    )(h, gain, w_in)


def _pool_band_matrices(tm):
    t = np.arange(tm)[:, None]
    j = np.arange(tm + 2 * HALO)[None, :] - HALO
    mats = [((j - t >= -(w // 2)) & (j - t <= w // 2 - 1)) for w in POOL_WINDOWS]
    return jnp.asarray(np.stack(mats).astype(np.float32), dtype=_BF16)


def _mixer_kernel(proj_ref, zprev_ref, znext_ref, h_ref, vg_ref, ws_ref, bs_ref, wpa_ref, band_ref,
                  wpool_ref, pscale_ref, wpb_ref, wout_ref, o_ref, a_scr, b_scr, *, tm, seq_len):
    tiles_per_seq = seq_len // tm
    tile_in_seq = pl.program_id(0) % tiles_per_seq

    z_tile = proj_ref[:, Z_OFF:Z_OFF + B_WIDTH]
    z_prev = jnp.where(tile_in_seq == 0, jnp.zeros_like(zprev_ref), zprev_ref[...])
    z_next = jnp.where(tile_in_seq == tiles_per_seq - 1, jnp.zeros_like(znext_ref), znext_ref[...])
    z_ext = jnp.concatenate([z_prev, z_tile, z_next], axis=0)
    gcols = [slice(g * POOL_GROUP_DIM, (g + 1) * POOL_GROUP_DIM) for g in range(POOL_GROUPS)]
    win_sums = [_dot(band_ref[g], z_ext[:, gcols[g]]) for g in range(POOL_GROUPS)]

    v = proj_ref[:, A_WIDTH:2 * A_WIDTH].astype(_F32)
    vn = _rms_norm(v, vg_ref[...]).astype(_BF16)
    for c in range(tm // CHUNK):
        rows = slice(c * CHUNK, (c + 1) * CHUNK)
        for hd in range(A_HEADS):
            cols = slice(hd * A_HEAD_DIM, (hd + 1) * A_HEAD_DIM)
            mixed = _dot(ws_ref[hd], vn[rows, cols]) + bs_ref[hd]
            a_scr[rows, cols] = (proj_ref[rows, cols].astype(_F32) * mixed).astype(_BF16)

    t = tile_in_seq * tm + lax.broadcasted_iota(jnp.int32, (tm, 1), 0)
    pooled = []
    for g, w in enumerate(POOL_WINDOWS):
        half = w // 2
        count = (jnp.minimum(t + half, seq_len) - jnp.maximum(t - half, 0)).astype(_F32)
        pooled.append((win_sums[g] / count - z_tile[:, gcols[g]].astype(_F32)).astype(_BF16))
    y_a = _dot(a_scr[...], wpa_ref[...])
    for g in range(POOL_GROUPS):
        mixed_b = _dot(pooled[g], wpool_ref[g]) * pscale_ref[:, gcols[g]]
        b_scr[:, gcols[g]] = mixed_b.astype(_BF16)
    y_b = _dot(b_scr[...], wpb_ref[...])

    ga = proj_ref[:, GA_OFF:GA_OFF + D_MODEL].astype(_F32)
    gb = proj_ref[:, GB_OFF:GB_OFF + D_MODEL].astype(_F32)
    merged = jax.nn.sigmoid(ga) * y_a + jax.nn.sigmoid(gb) * y_b
    o_ref[...] = h_ref[...] + _dot(merged.astype(_BF16), wout_ref[...])


def _mixer(proj, h, v_gain, w_spatial, b_spatial, w_proj_a, w_pool, pool_scale, w_proj_b, w_out, *,
           seq_len):
    t = h.shape[0]
    tm = MIX_TM
    halo_blocks_per_tile = tm // HALO
    last_halo_block = t // HALO - 1
    z_block = Z_OFF // B_WIDTH

    def const(shape):
        return pl.BlockSpec(shape, lambda i: (0,) * len(shape), pipeline_mode=pl.Buffered(1))

    return pl.pallas_call(
        functools.partial(_mixer_kernel, tm=tm, seq_len=seq_len),
        grid=(t // tm,),
        in_specs=[
            pl.BlockSpec((tm, IN_WIDTH), lambda i: (i, 0)),
            pl.BlockSpec((HALO, B_WIDTH),
                         lambda i: (jnp.maximum(i * halo_blocks_per_tile - 1, 0), z_block)),
            pl.BlockSpec((HALO, B_WIDTH),
                         lambda i: (jnp.minimum((i + 1) * halo_blocks_per_tile, last_halo_block), z_block)),
            pl.BlockSpec((tm, D_MODEL), lambda i: (i, 0)),
            const((1, A_WIDTH)),
            const((A_HEADS, CHUNK, CHUNK)),
            const((A_HEADS, CHUNK, 1)),
            const((A_WIDTH, D_MODEL)),
            const((POOL_GROUPS, tm, tm + 2 * HALO)),
            const((POOL_GROUPS, POOL_GROUP_DIM, POOL_GROUP_DIM)),
            const((1, B_WIDTH)),
            const((B_WIDTH, D_MODEL)),
            const((D_MODEL, D_MODEL)),
        ],
        out_specs=pl.BlockSpec((tm, D_MODEL), lambda i: (i, 0)),
        out_shape=jax.ShapeDtypeStruct((t, D_MODEL), _F32),
        scratch_shapes=[pltpu.VMEM((tm, A_WIDTH), _BF16), pltpu.VMEM((tm, B_WIDTH), _BF16)],
        compiler_params=pltpu.CompilerParams(
            dimension_semantics=("arbitrary",),
            vmem_limit_bytes=VMEM_LIMIT_BYTES),
        name="mixer",
    )(proj, proj, proj, h, v_gain, w_spatial, b_spatial, w_proj_a, _pool_band_matrices(tm),
      w_pool, pool_scale, w_proj_b, w_out)


def kernel(x_prompt, x_sample, ffn1_norm, ffn1_w1, ffn1_w3, ffn1_w2, mix_norm, w_in, v_norm,
           w_spatial, b_spatial, w_proj_a, w_pool, pool_scale, w_proj_b, w_out,
           ffn2_norm, ffn2_w1, ffn2_w3, ffn2_w2, final_norm):
    depth = ffn1_w1.shape[0]
    assert depth >= 1, "the output norm is fused into the last layer's second FFN"
    bf = lambda w: w.astype(_BF16)
    rows = lambda v: v.reshape(v.shape[0], 1, -1)
    final_gain = final_norm.reshape(1, -1)
    ffn1 = (rows(ffn1_norm), bf(ffn1_w1), bf(ffn1_w3), bf(ffn1_w2))
    ffn2 = (rows(ffn2_norm), bf(ffn2_w1), bf(ffn2_w3), bf(ffn2_w2))
    proj_p = (rows(mix_norm), bf(w_in))
    mix_p = (rows(v_norm), bf(w_spatial), b_spatial[..., None], bf(w_proj_a), bf(w_pool),
             rows(pool_scale), bf(w_proj_b), bf(w_out))

    outs = []
    for x in (x_prompt, x_sample):
        batch, seq_len, _ = x.shape
        h = x.reshape(batch * seq_len, D_MODEL)
        for l in range(depth):
            h = _ffn(h, *(p[l] for p in ffn1), final_gain, final_norm=False)
            proj = _proj(h, *(p[l] for p in proj_p))
            h = _mixer(proj, h, *(p[l] for p in mix_p), seq_len=seq_len)
            h = _ffn(h, *(p[l] for p in ffn2), final_gain, final_norm=(l == depth - 1))
        outs.append(h.reshape(batch, seq_len, D_MODEL))
    return tuple(outs)
```

```python
import functools

import jax
import jax.numpy as jnp
import numpy as np
from jax import lax
from jax.experimental import pallas as pl
from jax.experimental.pallas import tpu as pltpu

EPS = 1e-6
D_MODEL = 2048
D_FF = 5632
CHUNK = 128
A_HEADS = 8
A_WIDTH = D_MODEL // 2
A_HEAD_DIM = A_WIDTH // A_HEADS
POOL_WINDOWS = (2, 4, 8, 16)
POOL_GROUPS = len(POOL_WINDOWS)
B_WIDTH = D_MODEL // 2
POOL_GROUP_DIM = B_WIDTH // POOL_GROUPS
IN_WIDTH = 2 * A_WIDTH + B_WIDTH + 2 * D_MODEL
Z_OFF = 2 * A_WIDTH
GA_OFF = Z_OFF + B_WIDTH
GB_OFF = GA_OFF + D_MODEL

BF16_SUBLANES = 16
HALO = BF16_SUBLANES
VMEM_LIMIT_BYTES = 60 * 1024 * 1024

FFN_TM = 1024
FFN_TF = 512
PROJ_TM = 1024
PROJ_TN = 1792
MIX_TM = 256

_F32 = jnp.float32
_BF16 = jnp.bfloat16


def _rms_norm(x, gain):
    return x * lax.rsqrt(jnp.mean(x * x, axis=-1, keepdims=True) + EPS) * gain


def _sigmoid(x):
    return 0.5 * jnp.tanh(0.5 * x) + 0.5


def _dot(a, b):
    return jnp.dot(a, b, preferred_element_type=_F32)


def _ffn_kernel(x_ref, g_ref, w1_ref, w3_ref, w2_ref, go_ref, o_ref, n_scr, *, final_norm):
    f = pl.program_id(1)

    def block_update():
        n = n_scr[...]
        h1 = _dot(n, w1_ref[...])
        h3 = _dot(n, w3_ref[...])
        half_h1 = 0.5 * h1
        gated = (half_h1 * (jnp.tanh(half_h1) + 1.0) * (0.5 * h3)).astype(_BF16)
        return _dot(gated, w2_ref[...])

    @pl.when(f == 0)
    def _():
        x = x_ref[...]
        n_scr[...] = _rms_norm(x, g_ref[...]).astype(_BF16)
        o_ref[...] = x + block_update()

    @pl.when(f != 0)
    def _():
        o_ref[...] += block_update()

    if final_norm:
        @pl.when(f == pl.num_programs(1) - 1)
        def _():
            o_ref[...] = _rms_norm(o_ref[...], go_ref[...])


def _ffn(x, gain, w1, w3, w2, out_gain, *, final_norm):
    t = x.shape[0]
    tm, tf = FFN_TM, FFN_TF
    return pl.pallas_call(
        functools.partial(_ffn_kernel, final_norm=final_norm),
        grid=(t // tm, D_FF // tf),
        in_specs=[
            pl.BlockSpec((tm, D_MODEL), lambda i, f: (i, 0)),
            pl.BlockSpec((1, D_MODEL), lambda i, f: (0, 0)),
            pl.BlockSpec((D_MODEL, tf), lambda i, f: (0, f)),
            pl.BlockSpec((D_MODEL, tf), lambda i, f: (0, f)),
            pl.BlockSpec((tf, D_MODEL), lambda i, f: (f, 0)),
            pl.BlockSpec((1, D_MODEL), lambda i, f: (0, 0)),
        ],
        out_specs=pl.BlockSpec((tm, D_MODEL), lambda i, f: (i, 0)),
        out_shape=jax.ShapeDtypeStruct((t, D_MODEL), _F32),
        scratch_shapes=[pltpu.VMEM((tm, D_MODEL), _BF16)],
        compiler_params=pltpu.CompilerParams(
            dimension_semantics=("arbitrary", "arbitrary"),
            vmem_limit_bytes=VMEM_LIMIT_BYTES),
        name="ffn_final" if final_norm else "ffn",
    )(x, gain, w1, w3, w2, out_gain)


def _proj_kernel(h_ref, g_ref, w_ref, o_ref, n_scr):
    @pl.when(pl.program_id(1) == 0)
    def _():
        n_scr[...] = _rms_norm(h_ref[...], g_ref[...]).astype(_BF16)

    o_ref[...] = _dot(n_scr[...], w_ref[...]).astype(_BF16)


def _proj(h, gain, w_in):
    t = h.shape[0]
    tm, tn = PROJ_TM, PROJ_TN
    return pl.pallas_call(
        _proj_kernel,
        grid=(t // tm, IN_WIDTH // tn),
        in_specs=[
            pl.BlockSpec((tm, D_MODEL), lambda i, j: (i, 0)),
            pl.BlockSpec((1, D_MODEL), lambda i, j: (0, 0)),
            pl.BlockSpec((D_MODEL, tn), lambda i, j: (0, j)),
        ],
        out_specs=pl.BlockSpec((tm, tn), lambda i, j: (i, j)),
        out_shape=jax.ShapeDtypeStruct((t, IN_WIDTH), _BF16),
        scratch_shapes=[pltpu.VMEM((tm, D_MODEL), _BF16)],
        compiler_params=pltpu.CompilerParams(
            dimension_semantics=("arbitrary", "arbitrary"),
            vmem_limit_bytes=VMEM_LIMIT_BYTES),
        name="proj",
    )(h, gain, w_in)


def _pool_band_matrices(tm):
    t = np.arange(tm)[:, None]
    j = np.arange(tm + 2 * HALO)[None, :] - HALO
    mats = [((j - t >= -(w // 2)) & (j - t <= w // 2 - 1)) for w in POOL_WINDOWS]
    return jnp.asarray(np.stack(mats).astype(np.float32), dtype=_BF16)


def _mixer_kernel(proj_ref, zprev_ref, znext_ref, h_ref, vg_ref, ws_ref, bs_ref, wpa_ref, band_ref,
                  wpool_ref, pscale_ref, wpb_ref, wout_ref, o_ref, a_scr, b_scr, *, tm, seq_len):
    tiles_per_seq = seq_len // tm
    tile_in_seq = pl.program_id(0) % tiles_per_seq

    z_tile = proj_ref[:, Z_OFF:Z_OFF + B_WIDTH]
    z_prev = jnp.where(tile_in_seq == 0, jnp.zeros_like(zprev_ref), zprev_ref[...])
    z_next = jnp.where(tile_in_seq == tiles_per_seq - 1, jnp.zeros_like(znext_ref), znext_ref[...])
    z_ext = jnp.concatenate([z_prev, z_tile, z_next], axis=0)
    gcols = [slice(g * POOL_GROUP_DIM, (g + 1) * POOL_GROUP_DIM) for g in range(POOL_GROUPS)]
    win_sums = [_dot(band_ref[g], z_ext[:, gcols[g]]) for g in range(POOL_GROUPS)]

    v = proj_ref[:, A_WIDTH:2 * A_WIDTH].astype(_F32)
    vn = _rms_norm(v, vg_ref[...]).astype(_BF16)
    for c in range(tm // CHUNK):
        rows = slice(c * CHUNK, (c + 1) * CHUNK)
        for hd in range(A_HEADS):
            cols = slice(hd * A_HEAD_DIM, (hd + 1) * A_HEAD_DIM)
            mixed = _dot(ws_ref[hd], vn[rows, cols]) + bs_ref[hd]
            a_scr[rows, cols] = (proj_ref[rows, cols].astype(_F32) * mixed).astype(_BF16)

    t = tile_in_seq * tm + lax.broadcasted_iota(jnp.int32, (tm, 1), 0)
    pooled = []
    for g, w in enumerate(POOL_WINDOWS):
        half = w // 2
        count = (jnp.minimum(t + half, seq_len) - jnp.maximum(t - half, 0)).astype(_F32)
        pooled.append((win_sums[g] / count - z_tile[:, gcols[g]].astype(_F32)).astype(_BF16))
    y_a = _dot(a_scr[...], wpa_ref[...])
    for g in range(POOL_GROUPS):
        mixed_b = _dot(pooled[g], wpool_ref[g]) * pscale_ref[:, gcols[g]]
        b_scr[:, gcols[g]] = mixed_b.astype(_BF16)
    y_b = _dot(b_scr[...], wpb_ref[...])

    ga = proj_ref[:, GA_OFF:GA_OFF + D_MODEL].astype(_F32)
    gb = proj_ref[:, GB_OFF:GB_OFF + D_MODEL].astype(_F32)
    merged = _sigmoid(ga) * y_a + _sigmoid(gb) * y_b
    o_ref[...] = h_ref[...] + _dot(merged.astype(_BF16), wout_ref[...])


def _mixer(proj, h, v_gain, w_spatial, b_spatial, w_proj_a, w_pool, pool_scale, w_proj_b, w_out, *,
           seq_len):
    t = h.shape[0]
    tm = MIX_TM
    halo_blocks_per_tile = tm // HALO
    last_halo_block = t // HALO - 1
    z_block = Z_OFF // B_WIDTH

    def const(shape):
        return pl.BlockSpec(shape, lambda i: (0,) * len(shape), pipeline_mode=pl.Buffered(1))

    return pl.pallas_call(
        functools.partial(_mixer_kernel, tm=tm, seq_len=seq_len),
        grid=(t // tm,),
        in_specs=[
            pl.BlockSpec((tm, IN_WIDTH), lambda i: (i, 0)),
            pl.BlockSpec((HALO, B_WIDTH),
                         lambda i: (jnp.maximum(i * halo_blocks_per_tile - 1, 0), z_block)),
            pl.BlockSpec((HALO, B_WIDTH),
                         lambda i: (jnp.minimum((i + 1) * halo_blocks_per_tile, last_halo_block), z_block)),
            pl.BlockSpec((tm, D_MODEL), lambda i: (i, 0)),
            const((1, A_WIDTH)),
            const((A_HEADS, CHUNK, CHUNK)),
            const((A_HEADS, CHUNK, 1)),
            const((A_WIDTH, D_MODEL)),
            const((POOL_GROUPS, tm, tm + 2 * HALO)),
            const((POOL_GROUPS, POOL_GROUP_DIM, POOL_GROUP_DIM)),
            const((1, B_WIDTH)),
            const((B_WIDTH, D_MODEL)),
            const((D_MODEL, D_MODEL)),
        ],
        out_specs=pl.BlockSpec((tm, D_MODEL), lambda i: (i, 0)),
        out_shape=jax.ShapeDtypeStruct((t, D_MODEL), _F32),
        scratch_shapes=[pltpu.VMEM((tm, A_WIDTH), _BF16), pltpu.VMEM((tm, B_WIDTH), _BF16)],
        compiler_params=pltpu.CompilerParams(
            dimension_semantics=("arbitrary",),
            vmem_limit_bytes=VMEM_LIMIT_BYTES),
        name="mixer",
    )(proj, proj, proj, h, v_gain, w_spatial, b_spatial, w_proj_a, _pool_band_matrices(tm),
      w_pool, pool_scale, w_proj_b, w_out)


def kernel(x_prompt, x_sample, ffn1_norm, ffn1_w1, ffn1_w3, ffn1_w2, mix_norm, w_in, v_norm,
           w_spatial, b_spatial, w_proj_a, w_pool, pool_scale, w_proj_b, w_out,
           ffn2_norm, ffn2_w1, ffn2_w3, ffn2_w2, final_norm):
    depth = ffn1_w1.shape[0]
    assert depth >= 1, "the output norm is fused into the last layer's second FFN"
    bf = lambda w: w.astype(_BF16)
    rows = lambda v: v.reshape(v.shape[0], 1, -1)
    final_gain = final_norm.reshape(1, -1)
    ffn1 = (rows(ffn1_norm), bf(ffn1_w1), bf(ffn1_w3), bf(ffn1_w2))
    ffn2 = (rows(ffn2_norm), bf(ffn2_w1), bf(ffn2_w3), bf(ffn2_w2))
    proj_p = (rows(mix_norm), bf(w_in))
    mix_p = (rows(v_norm), bf(w_spatial), b_spatial[..., None], bf(w_proj_a), bf(w_pool),
             rows(pool_scale), bf(w_proj_b), bf(w_out))

    outs = []
    for x in (x_prompt, x_sample):
        batch, seq_len, _ = x.shape
        h = x.reshape(batch * seq_len, D_MODEL)
        for l in range(depth):
            h = _ffn(h, *(p[l] for p in ffn1), final_gain, final_norm=False)
            proj = _proj(h, *(p[l] for p in proj_p))
            h = _mixer(proj, h, *(p[l] for p in mix_p), seq_len=seq_len)
            h = _ffn(h, *(p[l] for p in ffn2), final_gain, final_norm=(l == depth - 1))
        outs.append(h.reshape(batch, seq_len, D_MODEL))
    return tuple(outs)
```

```python
import functools

import jax
import jax.numpy as jnp
import numpy as np
from jax import lax
from jax.experimental import pallas as pl
from jax.experimental.pallas import tpu as pltpu

EPS = 1e-6
D_MODEL = 2048
D_FF = 5632
CHUNK = 128
A_HEADS = 8
A_WIDTH = D_MODEL // 2
A_HEAD_DIM = A_WIDTH // A_HEADS
POOL_WINDOWS = (2, 4, 8, 16)
POOL_GROUPS = len(POOL_WINDOWS)
B_WIDTH = D_MODEL // 2
POOL_GROUP_DIM = B_WIDTH // POOL_GROUPS
IN_WIDTH = 2 * A_WIDTH + B_WIDTH + 2 * D_MODEL
Z_OFF = 2 * A_WIDTH
GA_OFF = Z_OFF + B_WIDTH
GB_OFF = GA_OFF + D_MODEL

BF16_SUBLANES = 16
HALO = BF16_SUBLANES
VMEM_LIMIT_BYTES = 60 * 1024 * 1024

FFN_TM = 1024
FFN_TF = 512
PROJ_TM = 1024
PROJ_TN = 1792
MIX_TM = 256

_F32 = jnp.float32
_BF16 = jnp.bfloat16


def _rms_norm(x, gain):
    return x * lax.rsqrt(jnp.mean(x * x, axis=-1, keepdims=True) + EPS) * gain


def _sigmoid(x):
    return 0.5 * jnp.tanh(0.5 * x) + 0.5


def _dot(a, b):
    return jnp.dot(a, b, preferred_element_type=_F32)


def _ffn_kernel(x_ref, g_ref, w13_ref, w2_ref, go_ref, o_ref, n_scr, *, tf, final_norm):
    f = pl.program_id(1)

    def block_update():
        h13 = _dot(n_scr[...], w13_ref[...])
        h1, h3 = h13[:, :tf], h13[:, tf:]
        half_h1 = 0.5 * h1
        gated = (half_h1 * (jnp.tanh(half_h1) + 1.0) * (0.5 * h3)).astype(_BF16)
        return _dot(gated, w2_ref[...])

    @pl.when(f == 0)
    def _():
        x = x_ref[...]
        n_scr[...] = _rms_norm(x, g_ref[...]).astype(_BF16)
        o_ref[...] = x + block_update()

    @pl.when(f != 0)
    def _():
        o_ref[...] += block_update()

    if final_norm:
        @pl.when(f == pl.num_programs(1) - 1)
        def _():
            o_ref[...] = _rms_norm(o_ref[...], go_ref[...])


def _interleave_blocks(w1, w3, tf):
    *lead, d, f_dim = w1.shape
    pair = jnp.stack([w1.reshape(*lead, d, f_dim // tf, tf), w3.reshape(*lead, d, f_dim // tf, tf)],
                     axis=-2)
    return pair.reshape(*lead, d, 2 * f_dim)


def _ffn(x, gain, w13, w2, out_gain, *, final_norm):
    t = x.shape[0]
    tm, tf = FFN_TM, FFN_TF
    return pl.pallas_call(
        functools.partial(_ffn_kernel, tf=tf, final_norm=final_norm),
        grid=(t // tm, D_FF // tf),
        in_specs=[
            pl.BlockSpec((tm, D_MODEL), lambda i, f: (i, 0)),
            pl.BlockSpec((1, D_MODEL), lambda i, f: (0, 0)),
            pl.BlockSpec((D_MODEL, 2 * tf), lambda i, f: (0, f)),
            pl.BlockSpec((tf, D_MODEL), lambda i, f: (f, 0)),
            pl.BlockSpec((1, D_MODEL), lambda i, f: (0, 0)),
        ],
        out_specs=pl.BlockSpec((tm, D_MODEL), lambda i, f: (i, 0)),
        out_shape=jax.ShapeDtypeStruct((t, D_MODEL), _F32),
        scratch_shapes=[pltpu.VMEM((tm, D_MODEL), _BF16)],
        compiler_params=pltpu.CompilerParams(
            dimension_semantics=("arbitrary", "arbitrary"),
            vmem_limit_bytes=VMEM_LIMIT_BYTES),
        name="ffn_final" if final_norm else "ffn",
    )(x, gain, w13, w2, out_gain)


def _proj_kernel(h_ref, g_ref, w_ref, o_ref, n_scr):
    def project():
        o_ref[...] = _dot(n_scr[...], w_ref[...]).astype(_BF16)

    @pl.when(pl.program_id(1) == 0)
    def _():
        n_scr[...] = _rms_norm(h_ref[...], g_ref[...]).astype(_BF16)
        project()

    @pl.when(pl.program_id(1) != 0)
    def _():
        project()


def _proj(h, gain, w_in):
    t = h.shape[0]
    tm, tn = PROJ_TM, PROJ_TN
    return pl.pallas_call(
        _proj_kernel,
        grid=(t // tm, IN_WIDTH // tn),
        in_specs=[
            pl.BlockSpec((tm, D_MODEL), lambda i, j: (i, 0)),
            pl.BlockSpec((1, D_MODEL), lambda i, j: (0, 0)),
            pl.BlockSpec((D_MODEL, tn), lambda i, j: (0, j)),
        ],
        out_specs=pl.BlockSpec((tm, tn), lambda i, j: (i, j)),
        out_shape=jax.ShapeDtypeStruct((t, IN_WIDTH), _BF16),
        scratch_shapes=[pltpu.VMEM((tm, D_MODEL), _BF16)],
        compiler_params=pltpu.CompilerParams(
            dimension_semantics=("arbitrary", "arbitrary"),
            vmem_limit_bytes=VMEM_LIMIT_BYTES),
        name="proj",
    )(h, gain, w_in)


def _pool_band_matrices(tm):
    t = np.arange(tm)[:, None]
    j = np.arange(tm + 2 * HALO)[None, :] - HALO
    mats = [((j - t >= -(w // 2)) & (j - t <= w // 2 - 1)) for w in POOL_WINDOWS]
    return jnp.asarray(np.stack(mats).astype(np.float32), dtype=_BF16)


def _mixer_kernel(proj_ref, zprev_ref, znext_ref, h_ref, vg_ref, ws_ref, bs_ref, wpa_ref, band_ref,
                  wpool_ref, pscale_ref, wpb_ref, wout_ref, o_ref, a_scr, b_scr, *, tm, seq_len):
    tiles_per_seq = seq_len // tm
    tile_in_seq = pl.program_id(0) % tiles_per_seq

    z_tile = proj_ref[:, Z_OFF:Z_OFF + B_WIDTH]
    z_prev = jnp.where(tile_in_seq == 0, jnp.zeros_like(zprev_ref), zprev_ref[...])
    z_next = jnp.where(tile_in_seq == tiles_per_seq - 1, jnp.zeros_like(znext_ref), znext_ref[...])
    z_ext = jnp.concatenate([z_prev, z_tile, z_next], axis=0)
    gcols = [slice(g * POOL_GROUP_DIM, (g + 1) * POOL_GROUP_DIM) for g in range(POOL_GROUPS)]
    win_sums = [_dot(band_ref[g], z_ext[:, gcols[g]]) for g in range(POOL_GROUPS)]

    v = proj_ref[:, A_WIDTH:2 * A_WIDTH].astype(_F32)
    vn = _rms_norm(v, vg_ref[...]).astype(_BF16)
    for c in range(tm // CHUNK):
        rows = slice(c * CHUNK, (c + 1) * CHUNK)
        for hd in range(A_HEADS):
            cols = slice(hd * A_HEAD_DIM, (hd + 1) * A_HEAD_DIM)
            mixed = _dot(ws_ref[hd], vn[rows, cols]) + bs_ref[hd]
            a_scr[rows, cols] = (proj_ref[rows, cols].astype(_F32) * mixed).astype(_BF16)

    t = tile_in_seq * tm + lax.broadcasted_iota(jnp.int32, (tm, 1), 0)
    pooled = []
    for g, w in enumerate(POOL_WINDOWS):
        half = w // 2
        count = (jnp.minimum(t + half, seq_len) - jnp.maximum(t - half, 0)).astype(_F32)
        pooled.append((win_sums[g] / count - z_tile[:, gcols[g]].astype(_F32)).astype(_BF16))
    y_a = _dot(a_scr[...], wpa_ref[...])
    for g in range(POOL_GROUPS):
        mixed_b = _dot(pooled[g], wpool_ref[g]) * pscale_ref[:, gcols[g]]
        b_scr[:, gcols[g]] = mixed_b.astype(_BF16)
    y_b = _dot(b_scr[...], wpb_ref[...])

    ga = proj_ref[:, GA_OFF:GA_OFF + D_MODEL].astype(_F32)
    gb = proj_ref[:, GB_OFF:GB_OFF + D_MODEL].astype(_F32)
    merged = _sigmoid(ga) * y_a + _sigmoid(gb) * y_b
    o_ref[...] = h_ref[...] + _dot(merged.astype(_BF16), wout_ref[...])


def _mixer(proj, h, v_gain, w_spatial, b_spatial, w_proj_a, w_pool, pool_scale, w_proj_b, w_out, *,
           seq_len):
    t = h.shape[0]
    tm = MIX_TM
    halo_blocks_per_tile = tm // HALO
    last_halo_block = t // HALO - 1
    z_block = Z_OFF // B_WIDTH

    def const(shape):
        return pl.BlockSpec(shape, lambda i: (0,) * len(shape), pipeline_mode=pl.Buffered(1))

    return pl.pallas_call(
        functools.partial(_mixer_kernel, tm=tm, seq_len=seq_len),
        grid=(t // tm,),
        in_specs=[
            pl.BlockSpec((tm, IN_WIDTH), lambda i: (i, 0)),
            pl.BlockSpec((HALO, B_WIDTH),
                         lambda i: (jnp.maximum(i * halo_blocks_per_tile - 1, 0), z_block)),
            pl.BlockSpec((HALO, B_WIDTH),
                         lambda i: (jnp.minimum((i + 1) * halo_blocks_per_tile, last_halo_block), z_block)),
            pl.BlockSpec((tm, D_MODEL), lambda i: (i, 0)),
            const((1, A_WIDTH)),
            const((A_HEADS, CHUNK, CHUNK)),
            const((A_HEADS, CHUNK, 1)),
            const((A_WIDTH, D_MODEL)),
            const((POOL_GROUPS, tm, tm + 2 * HALO)),
            const((POOL_GROUPS, POOL_GROUP_DIM, POOL_GROUP_DIM)),
            const((1, B_WIDTH)),
            const((B_WIDTH, D_MODEL)),
            const((D_MODEL, D_MODEL)),
        ],
        out_specs=pl.BlockSpec((tm, D_MODEL), lambda i: (i, 0)),
        out_shape=jax.ShapeDtypeStruct((t, D_MODEL), _F32),
        scratch_shapes=[pltpu.VMEM((tm, A_WIDTH), _BF16), pltpu.VMEM((tm, B_WIDTH), _BF16)],
        compiler_params=pltpu.CompilerParams(
            dimension_semantics=("arbitrary",),
            vmem_limit_bytes=VMEM_LIMIT_BYTES),
        name="mixer",
    )(proj, proj, proj, h, v_gain, w_spatial, b_spatial, w_proj_a, _pool_band_matrices(tm),
      w_pool, pool_scale, w_proj_b, w_out)


def kernel(x_prompt, x_sample, ffn1_norm, ffn1_w1, ffn1_w3, ffn1_w2, mix_norm, w_in, v_norm,
           w_spatial, b_spatial, w_proj_a, w_pool, pool_scale, w_proj_b, w_out,
           ffn2_norm, ffn2_w1, ffn2_w3, ffn2_w2, final_norm):
    depth = ffn1_w1.shape[0]
    assert depth >= 1, "the output norm is fused into the last layer's second FFN"
    bf = lambda w: w.astype(_BF16)
    rows = lambda v: v.reshape(v.shape[0], 1, -1)
    final_gain = final_norm.reshape(1, -1)
    ffn1 = (rows(ffn1_norm), _interleave_blocks(bf(ffn1_w1), bf(ffn1_w3), FFN_TF), bf(ffn1_w2))
    ffn2 = (rows(ffn2_norm), _interleave_blocks(bf(ffn2_w1), bf(ffn2_w3), FFN_TF), bf(ffn2_w2))
    proj_p = (rows(mix_norm), bf(w_in))
    mix_p = (rows(v_norm), bf(w_spatial), b_spatial[..., None], bf(w_proj_a), bf(w_pool),
             rows(pool_scale), bf(w_proj_b), bf(w_out))

    outs = []
    for x in (x_prompt, x_sample):
        batch, seq_len, _ = x.shape
        h = x.reshape(batch * seq_len, D_MODEL)
        for l in range(depth):
            h = _ffn(h, *(p[l] for p in ffn1), final_gain, final_norm=False)
            proj = _proj(h, *(p[l] for p in proj_p))
            h = _mixer(proj, h, *(p[l] for p in mix_p), seq_len=seq_len)
            h = _ffn(h, *(p[l] for p in ffn2), final_gain, final_norm=(l == depth - 1))
        outs.append(h.reshape(batch, seq_len, D_MODEL))
    return tuple(outs)
```

```python
import functools

import jax
import jax.numpy as jnp
import numpy as np
from jax import lax
from jax.experimental import pallas as pl
from jax.experimental.pallas import tpu as pltpu

EPS = 1e-6
D_MODEL = 2048
D_FF = 5632
CHUNK = 128
A_HEADS = 8
A_WIDTH = D_MODEL // 2
A_HEAD_DIM = A_WIDTH // A_HEADS
POOL_WINDOWS = (2, 4, 8, 16)
POOL_GROUPS = len(POOL_WINDOWS)
B_WIDTH = D_MODEL // 2
POOL_GROUP_DIM = B_WIDTH // POOL_GROUPS
IN_WIDTH = 2 * A_WIDTH + B_WIDTH + 2 * D_MODEL
Z_OFF = 2 * A_WIDTH
GA_OFF = Z_OFF + B_WIDTH
GB_OFF = GA_OFF + D_MODEL

BF16_SUBLANES = 16
HALO = BF16_SUBLANES
VMEM_LIMIT_BYTES = 60 * 1024 * 1024

FFN_TM = 1024
FFN_TF = 512
PROJ_TM = 1024
PROJ_TN = 1792
MIX_TM = 256

_F32 = jnp.float32
_BF16 = jnp.bfloat16


def _rms_norm(x, gain):
    return x * lax.rsqrt(jnp.mean(x * x, axis=-1, keepdims=True) + EPS) * gain


def _sigmoid(x):
    return 0.5 * jnp.tanh(0.5 * x) + 0.5


def _dot(a, b):
    return jnp.dot(a, b, preferred_element_type=_F32)


def _ffn_kernel(x_ref, g_ref, w1_ref, w3_ref, w2_ref, go_ref, o_ref, n_scr, *, final_norm):
    f = pl.program_id(1)

    def block_update():
        n = n_scr[...]
        h1 = _dot(n, w1_ref[...])
        h3 = _dot(n, w3_ref[...])
        half_h1 = 0.5 * h1
        gated = (half_h1 * (jnp.tanh(half_h1) + 1.0) * (0.5 * h3)).astype(_BF16)
        return _dot(gated, w2_ref[...])

    @pl.when(f == 0)
    def _():
        x = x_ref[...]
        n_scr[...] = _rms_norm(x, g_ref[...]).astype(_BF16)
        o_ref[...] = x + block_update()

    last = pl.num_programs(1) - 1
    if final_norm:
        @pl.when((f != 0) & (f != last))
        def _():
            o_ref[...] += block_update()

        @pl.when(f == last)
        def _():
            o_ref[...] = _rms_norm(o_ref[...] + block_update(), go_ref[...])
    else:
        @pl.when(f != 0)
        def _():
            o_ref[...] += block_update()


def _ffn(x, gain, w1, w3, w2, out_gain, *, final_norm):
    t = x.shape[0]
    tm, tf = FFN_TM, FFN_TF
    return pl.pallas_call(
        functools.partial(_ffn_kernel, final_norm=final_norm),
        grid=(t // tm, D_FF // tf),
        in_specs=[
            pl.BlockSpec((tm, D_MODEL), lambda i, f: (i, 0)),
            pl.BlockSpec((1, D_MODEL), lambda i, f: (0, 0)),
            pl.BlockSpec((D_MODEL, tf), lambda i, f: (0, f)),
            pl.BlockSpec((D_MODEL, tf), lambda i, f: (0, f)),
            pl.BlockSpec((tf, D_MODEL), lambda i, f: (f, 0)),
            pl.BlockSpec((1, D_MODEL), lambda i, f: (0, 0)),
        ],
        out_specs=pl.BlockSpec((tm, D_MODEL), lambda i, f: (i, 0)),
        out_shape=jax.ShapeDtypeStruct((t, D_MODEL), _F32),
        scratch_shapes=[pltpu.VMEM((tm, D_MODEL), _BF16)],
        compiler_params=pltpu.CompilerParams(
            dimension_semantics=("arbitrary", "arbitrary"),
            vmem_limit_bytes=VMEM_LIMIT_BYTES),
        name="ffn_final" if final_norm else "ffn",
    )(x, gain, w1, w3, w2, out_gain)


def _proj_kernel(h_ref, g_ref, w_ref, o_ref, n_scr):
    def project():
        o_ref[...] = _dot(n_scr[...], w_ref[...]).astype(_BF16)

    @pl.when(pl.program_id(1) == 0)
    def _():
        n_scr[...] = _rms_norm(h_ref[...], g_ref[...]).astype(_BF16)
        project()

    @pl.when(pl.program_id(1) != 0)
    def _():
        project()


def _proj(h, gain, w_in):
    t = h.shape[0]
    tm, tn = PROJ_TM, PROJ_TN
    return pl.pallas_call(
        _proj_kernel,
        grid=(t // tm, IN_WIDTH // tn),
        in_specs=[
            pl.BlockSpec((tm, D_MODEL), lambda i, j: (i, 0)),
            pl.BlockSpec((1, D_MODEL), lambda i, j: (0, 0)),
            pl.BlockSpec((D_MODEL, tn), lambda i, j: (0, j)),
        ],
        out_specs=pl.BlockSpec((tm, tn), lambda i, j: (i, j)),
        out_shape=jax.ShapeDtypeStruct((t, IN_WIDTH), _BF16),
        scratch_shapes=[pltpu.VMEM((tm, D_MODEL), _BF16)],
        compiler_params=pltpu.CompilerParams(
            dimension_semantics=("arbitrary", "arbitrary"),
            vmem_limit_bytes=VMEM_LIMIT_BYTES),
        name="proj",
    )(h, gain, w_in)


def _pool_band_matrices(tm):
    t = np.arange(tm)[:, None]
    j = np.arange(tm + 2 * HALO)[None, :] - HALO
    mats = [((j - t >= -(w // 2)) & (j - t <= w // 2 - 1)) for w in POOL_WINDOWS]
    return jnp.asarray(np.stack(mats).astype(np.float32), dtype=_BF16)


def _mixer_kernel(proj_ref, zprev_ref, znext_ref, h_ref, vg_ref, ws_ref, bs_ref, wpa_ref, band_ref,
                  wpool_ref, pscale_ref, wpb_ref, wout_ref, o_ref, a_scr, b_scr, *, tm, seq_len):
    tiles_per_seq = seq_len // tm
    tile_in_seq = pl.program_id(0) % tiles_per_seq

    z_tile = proj_ref[:, Z_OFF:Z_OFF + B_WIDTH]
    z_prev = jnp.where(tile_in_seq == 0, jnp.zeros_like(zprev_ref), zprev_ref[...])
    z_next = jnp.where(tile_in_seq == tiles_per_seq - 1, jnp.zeros_like(znext_ref), znext_ref[...])
    z_ext = jnp.concatenate([z_prev, z_tile, z_next], axis=0)
    gcols = [slice(g * POOL_GROUP_DIM, (g + 1) * POOL_GROUP_DIM) for g in range(POOL_GROUPS)]
    win_sums = [_dot(band_ref[g], z_ext[:, gcols[g]]) for g in range(POOL_GROUPS)]

    v = proj_ref[:, A_WIDTH:2 * A_WIDTH].astype(_F32)
    vn = _rms_norm(v, vg_ref[...]).astype(_BF16)
    chunks = [slice(c * CHUNK, (c + 1) * CHUNK) for c in range(tm // CHUNK)]
    for hd in range(A_HEADS):
        cols = slice(hd * A_HEAD_DIM, (hd + 1) * A_HEAD_DIM)
        v_chunks = jnp.concatenate([vn[rows, cols] for rows in chunks], axis=1)
        mixed = _dot(ws_ref[hd], v_chunks) + bs_ref[hd]
        for c, rows in enumerate(chunks):
            u = proj_ref[rows, cols].astype(_F32)
            a_scr[rows, cols] = (u * mixed[:, c * A_HEAD_DIM:(c + 1) * A_HEAD_DIM]).astype(_BF16)

    t = tile_in_seq * tm + lax.broadcasted_iota(jnp.int32, (tm, 1), 0)
    pooled = []
    for g, w in enumerate(POOL_WINDOWS):
        half = w // 2
        count = (jnp.minimum(t + half, seq_len) - jnp.maximum(t - half, 0)).astype(_F32)
        pooled.append((win_sums[g] / count - z_tile[:, gcols[g]].astype(_F32)).astype(_BF16))
    y_a = _dot(a_scr[...], wpa_ref[...])
    for g in range(POOL_GROUPS):
        mixed_b = _dot(pooled[g], wpool_ref[g]) * pscale_ref[:, gcols[g]]
        b_scr[:, gcols[g]] = mixed_b.astype(_BF16)
    y_b = _dot(b_scr[...], wpb_ref[...])

    ga = proj_ref[:, GA_OFF:GA_OFF + D_MODEL].astype(_F32)
    gb = proj_ref[:, GB_OFF:GB_OFF + D_MODEL].astype(_F32)
    merged = _sigmoid(ga) * y_a + _sigmoid(gb) * y_b
    o_ref[...] = h_ref[...] + _dot(merged.astype(_BF16), wout_ref[...])


def _mixer(proj, h, v_gain, w_spatial, b_spatial, w_proj_a, w_pool, pool_scale, w_proj_b, w_out, *,
           seq_len):
    t = h.shape[0]
    tm = MIX_TM
    halo_blocks_per_tile = tm // HALO
    last_halo_block = t // HALO - 1
    z_block = Z_OFF // B_WIDTH

    def const(shape):
        return pl.BlockSpec(shape, lambda i: (0,) * len(shape), pipeline_mode=pl.Buffered(1))

    return pl.pallas_call(
        functools.partial(_mixer_kernel, tm=tm, seq_len=seq_len),
        grid=(t // tm,),
        in_specs=[
            pl.BlockSpec((tm, IN_WIDTH), lambda i: (i, 0)),
            pl.BlockSpec((HALO, B_WIDTH),
                         lambda i: (jnp.maximum(i * halo_blocks_per_tile - 1, 0), z_block)),
            pl.BlockSpec((HALO, B_WIDTH),
                         lambda i: (jnp.minimum((i + 1) * halo_blocks_per_tile, last_halo_block), z_block)),
            pl.BlockSpec((tm, D_MODEL), lambda i: (i, 0)),
            const((1, A_WIDTH)),
            const((A_HEADS, CHUNK, CHUNK)),
            const((A_HEADS, CHUNK, 1)),
            const((A_WIDTH, D_MODEL)),
            const((POOL_GROUPS, tm, tm + 2 * HALO)),
            const((POOL_GROUPS, POOL_GROUP_DIM, POOL_GROUP_DIM)),
            const((1, B_WIDTH)),
            const((B_WIDTH, D_MODEL)),
            const((D_MODEL, D_MODEL)),
        ],
        out_specs=pl.BlockSpec((tm, D_MODEL), lambda i: (i, 0)),
        out_shape=jax.ShapeDtypeStruct((t, D_MODEL), _F32),
        scratch_shapes=[pltpu.VMEM((tm, A_WIDTH), _BF16), pltpu.VMEM((tm, B_WIDTH), _BF16)],
        compiler_params=pltpu.CompilerParams(
            dimension_semantics=("arbitrary",),
            vmem_limit_bytes=VMEM_LIMIT_BYTES),
        name="mixer",
    )(proj, proj, proj, h, v_gain, w_spatial, b_spatial, w_proj_a, _pool_band_matrices(tm),
      w_pool, pool_scale, w_proj_b, w_out)


def kernel(x_prompt, x_sample, ffn1_norm, ffn1_w1, ffn1_w3, ffn1_w2, mix_norm, w_in, v_norm,
           w_spatial, b_spatial, w_proj_a, w_pool, pool_scale, w_proj_b, w_out,
           ffn2_norm, ffn2_w1, ffn2_w3, ffn2_w2, final_norm):
    depth = ffn1_w1.shape[0]
    assert depth >= 1, "the output norm is fused into the last layer's second FFN"
    bf = lambda w: w.astype(_BF16)
    rows = lambda v: v.reshape(v.shape[0], 1, -1)
    final_gain = final_norm.reshape(1, -1)
    ffn1 = (rows(ffn1_norm), bf(ffn1_w1), bf(ffn1_w3), bf(ffn1_w2))
    ffn2 = (rows(ffn2_norm), bf(ffn2_w1), bf(ffn2_w3), bf(ffn2_w2))
    proj_p = (rows(mix_norm), bf(w_in))
    mix_p = (rows(v_norm), bf(w_spatial), b_spatial[..., None], bf(w_proj_a), bf(w_pool),
             rows(pool_scale), bf(w_proj_b), bf(w_out))

    outs = []
    for x in (x_prompt, x_sample):
        batch, seq_len, _ = x.shape
        h = x.reshape(batch * seq_len, D_MODEL)
        for l in range(depth):
            h = _ffn(h, *(p[l] for p in ffn1), final_gain, final_norm=False)
            proj = _proj(h, *(p[l] for p in proj_p))
            h = _mixer(proj, h, *(p[l] for p in mix_p), seq_len=seq_len)
            h = _ffn(h, *(p[l] for p in ffn2), final_gain, final_norm=(l == depth - 1))
        outs.append(h.reshape(batch, seq_len, D_MODEL))
    return tuple(outs)
```

```python
import functools

import jax
import jax.numpy as jnp
import numpy as np
from jax import lax
from jax.experimental import pallas as pl
from jax.experimental.pallas import tpu as pltpu

EPS = 1e-6
D_MODEL = 2048
D_FF = 5632
CHUNK = 128
A_HEADS = 8
A_WIDTH = D_MODEL // 2
A_HEAD_DIM = A_WIDTH // A_HEADS
POOL_WINDOWS = (2, 4, 8, 16)
POOL_GROUPS = len(POOL_WINDOWS)
B_WIDTH = D_MODEL // 2
POOL_GROUP_DIM = B_WIDTH // POOL_GROUPS
IN_WIDTH = 2 * A_WIDTH + B_WIDTH + 2 * D_MODEL
Z_OFF = 2 * A_WIDTH
GA_OFF = Z_OFF + B_WIDTH
GB_OFF = GA_OFF + D_MODEL

BF16_SUBLANES = 16
HALO = BF16_SUBLANES
VMEM_LIMIT_BYTES = 60 * 1024 * 1024

FFN_TM = 1024
FFN_TF = 512
FFN_BLOCKS_PER_STEP = 2
PROJ_TM = 1024
PROJ_TN = 1792
MIX_TM = 256

_F32 = jnp.float32
_BF16 = jnp.bfloat16


def _rms_norm(x, gain):
    return x * lax.rsqrt(jnp.mean(x * x, axis=-1, keepdims=True) + EPS) * gain


def _sigmoid(x):
    return 0.5 * jnp.tanh(0.5 * x) + 0.5


def _dot(a, b):
    return jnp.dot(a, b, preferred_element_type=_F32)


def _ffn_kernel(x_hbm, g_ref, w1_ref, w3_ref, w2_ref, go_ref, o_hbm, acc, n_scr, x_sem, o_sem, *,
                tm, tf, n_blocks, final_norm):
    i, s = pl.program_id(0), pl.program_id(1)
    n_tiles, last = pl.num_programs(0), pl.num_programs(1) - 1
    slot = i % 2
    acc_ref = acc.at[slot]

    def x_copy(tile, sl):
        return pltpu.make_async_copy(x_hbm.at[pl.ds(tile * tm, tm)], acc.at[sl], x_sem.at[sl])

    def o_copy(tile, sl):
        return pltpu.make_async_copy(acc.at[sl], o_hbm.at[pl.ds(tile * tm, tm)], o_sem.at[sl])

    @pl.when((i == 0) & (s == 0))
    def _():
        x_copy(0, 0).start()

    @pl.when(s == 0)
    def _():
        x_copy(i, slot).wait()

    @pl.when((s == 1) & (i >= 1))
    def _():
        o_copy(i - 1, 1 - slot).wait()

    @pl.when((s == 1) & (i + 1 < n_tiles))
    def _():
        x_copy(i + 1, 1 - slot).start()

    def update(blocks):
        n = n_scr[...]
        total = None
        for b in range(blocks):
            cols = slice(b * tf, (b + 1) * tf)
            h1 = _dot(n, w1_ref[:, cols])
            h3 = _dot(n, w3_ref[:, cols])
            half_h1 = 0.5 * h1
            gated = (half_h1 * (jnp.tanh(half_h1) + 1.0) * (0.5 * h3)).astype(_BF16)
            part = _dot(gated, w2_ref[cols, :])
            total = part if total is None else total + part
        return total

    blocks_last = n_blocks - (pl.cdiv(n_blocks, FFN_BLOCKS_PER_STEP) - 1) * FFN_BLOCKS_PER_STEP

    @pl.when(s == 0)
    def _():
        x = acc_ref[...]
        n_scr[...] = _rms_norm(x, g_ref[...]).astype(_BF16)
        acc_ref[...] = x + update(FFN_BLOCKS_PER_STEP)

    @pl.when((s != 0) & (s != last))
    def _():
        acc_ref[...] += update(FFN_BLOCKS_PER_STEP)

    @pl.when(s == last)
    def _():
        h = acc_ref[...] + update(blocks_last)
        acc_ref[...] = _rms_norm(h, go_ref[...]) if final_norm else h
        o_copy(i, slot).start()

    @pl.when((s == last) & (i == n_tiles - 1))
    def _():
        o_copy(i, slot).wait()


def _ffn(x, gain, w1, w3, w2, out_gain, *, final_norm):
    t = x.shape[0]
    tm, tf = FFN_TM, FFN_TF
    n_blocks = D_FF // tf
    n_steps = pl.cdiv(n_blocks, FFN_BLOCKS_PER_STEP)
    assert n_steps >= 2, "the slot hand-over happens in step 1"
    step_cols = FFN_BLOCKS_PER_STEP * tf
    return pl.pallas_call(
        functools.partial(_ffn_kernel, tm=tm, tf=tf, n_blocks=n_blocks, final_norm=final_norm),
        grid=(t // tm, n_steps),
        in_specs=[
            pl.BlockSpec(memory_space=pl.ANY),
            pl.BlockSpec((1, D_MODEL), lambda i, s: (0, 0)),
            pl.BlockSpec((D_MODEL, step_cols), lambda i, s: (0, s)),
            pl.BlockSpec((D_MODEL, step_cols), lambda i, s: (0, s)),
            pl.BlockSpec((step_cols, D_MODEL), lambda i, s: (s, 0)),
            pl.BlockSpec((1, D_MODEL), lambda i, s: (0, 0)),
        ],
        out_specs=pl.BlockSpec(memory_space=pl.ANY),
        out_shape=jax.ShapeDtypeStruct((t, D_MODEL), _F32),
        scratch_shapes=[pltpu.VMEM((2, tm, D_MODEL), _F32),
                        pltpu.VMEM((tm, D_MODEL), _BF16),
                        pltpu.SemaphoreType.DMA((2,)),
                        pltpu.SemaphoreType.DMA((2,))],
        compiler_params=pltpu.CompilerParams(
            dimension_semantics=("arbitrary", "arbitrary"),
            vmem_limit_bytes=VMEM_LIMIT_BYTES),
        name="ffn_final" if final_norm else "ffn",
    )(x, gain, w1, w3, w2, out_gain)


def _proj_kernel(h_ref, g_ref, w_ref, o_ref, n_scr):
    def project():
        o_ref[...] = _dot(n_scr[...], w_ref[...]).astype(_BF16)

    @pl.when(pl.program_id(1) == 0)
    def _():
        n_scr[...] = _rms_norm(h_ref[...], g_ref[...]).astype(_BF16)
        project()

    @pl.when(pl.program_id(1) != 0)
    def _():
        project()


def _proj(h, gain, w_in):
    t = h.shape[0]
    tm, tn = PROJ_TM, PROJ_TN
    return pl.pallas_call(
        _proj_kernel,
        grid=(t // tm, IN_WIDTH // tn),
        in_specs=[
            pl.BlockSpec((tm, D_MODEL), lambda i, j: (i, 0)),
            pl.BlockSpec((1, D_MODEL), lambda i, j: (0, 0)),
            pl.BlockSpec((D_MODEL, tn), lambda i, j: (0, j)),
        ],
        out_specs=pl.BlockSpec((tm, tn), lambda i, j: (i, j)),
        out_shape=jax.ShapeDtypeStruct((t, IN_WIDTH), _BF16),
        scratch_shapes=[pltpu.VMEM((tm, D_MODEL), _BF16)],
        compiler_params=pltpu.CompilerParams(
            dimension_semantics=("arbitrary", "arbitrary"),
            vmem_limit_bytes=VMEM_LIMIT_BYTES),
        name="proj",
    )(h, gain, w_in)


def _pool_band_matrices(tm):
    t = np.arange(tm)[:, None]
    j = np.arange(tm + 2 * HALO)[None, :] - HALO
    mats = [((j - t >= -(w // 2)) & (j - t <= w // 2 - 1)) for w in POOL_WINDOWS]
    return jnp.asarray(np.stack(mats).astype(np.float32), dtype=_BF16)


def _mixer_kernel(proj_ref, zprev_ref, znext_ref, h_ref, vg_ref, ws_ref, bs_ref, wpa_ref, band_ref,
                  wpool_ref, pscale_ref, wpb_ref, wout_ref, o_ref, a_scr, b_scr, *, tm, seq_len):
    tiles_per_seq = seq_len // tm
    tile_in_seq = pl.program_id(0) % tiles_per_seq

    z_tile = proj_ref[:, Z_OFF:Z_OFF + B_WIDTH]
    z_prev = jnp.where(tile_in_seq == 0, jnp.zeros_like(zprev_ref), zprev_ref[...])
    z_next = jnp.where(tile_in_seq == tiles_per_seq - 1, jnp.zeros_like(znext_ref), znext_ref[...])
    z_ext = jnp.concatenate([z_prev, z_tile, z_next], axis=0)
    gcols = [slice(g * POOL_GROUP_DIM, (g + 1) * POOL_GROUP_DIM) for g in range(POOL_GROUPS)]
    win_sums = [_dot(band_ref[g], z_ext[:, gcols[g]]) for g in range(POOL_GROUPS)]

    v = proj_ref[:, A_WIDTH:2 * A_WIDTH].astype(_F32)
    vn = _rms_norm(v, vg_ref[...]).astype(_BF16)
    chunks = [slice(c * CHUNK, (c + 1) * CHUNK) for c in range(tm // CHUNK)]
    for hd in range(A_HEADS):
        cols = slice(hd * A_HEAD_DIM, (hd + 1) * A_HEAD_DIM)
        v_chunks = jnp.concatenate([vn[rows, cols] for rows in chunks], axis=1)
        mixed = _dot(ws_ref[hd], v_chunks) + bs_ref[hd]
        for c, rows in enumerate(chunks):
            u = proj_ref[rows, cols].astype(_F32)
            a_scr[rows, cols] = (u * mixed[:, c * A_HEAD_DIM:(c + 1) * A_HEAD_DIM]).astype(_BF16)

    t = tile_in_seq * tm + lax.broadcasted_iota(jnp.int32, (tm, 1), 0)
    pooled = []
    for g, w in enumerate(POOL_WINDOWS):
        half = w // 2
        count = (jnp.minimum(t + half, seq_len) - jnp.maximum(t - half, 0)).astype(_F32)
        pooled.append((win_sums[g] / count - z_tile[:, gcols[g]].astype(_F32)).astype(_BF16))
    y_a = _dot(a_scr[...], wpa_ref[...])
    for g in range(POOL_GROUPS):
        mixed_b = _dot(pooled[g], wpool_ref[g]) * pscale_ref[:, gcols[g]]
        b_scr[:, gcols[g]] = mixed_b.astype(_BF16)
    y_b = _dot(b_scr[...], wpb_ref[...])

    ga = proj_ref[:, GA_OFF:GA_OFF + D_MODEL].astype(_F32)
    gb = proj_ref[:, GB_OFF:GB_OFF + D_MODEL].astype(_F32)
    merged = _sigmoid(ga) * y_a + _sigmoid(gb) * y_b
    o_ref[...] = h_ref[...] + _dot(merged.astype(_BF16), wout_ref[...])


def _mixer(proj, h, v_gain, w_spatial, b_spatial, w_proj_a, w_pool, pool_scale, w_proj_b, w_out, *,
           seq_len):
    t = h.shape[0]
    tm = MIX_TM
    halo_blocks_per_tile = tm // HALO
    last_halo_block = t // HALO - 1
    z_block = Z_OFF // B_WIDTH

    def const(shape):
        return pl.BlockSpec(shape, lambda i: (0,) * len(shape), pipeline_mode=pl.Buffered(1))

    return pl.pallas_call(
        functools.partial(_mixer_kernel, tm=tm, seq_len=seq_len),
        grid=(t // tm,),
        in_specs=[
            pl.BlockSpec((tm, IN_WIDTH), lambda i: (i, 0)),
            pl.BlockSpec((HALO, B_WIDTH),
                         lambda i: (jnp.maximum(i * halo_blocks_per_tile - 1, 0), z_block)),
            pl.BlockSpec((HALO, B_WIDTH),
                         lambda i: (jnp.minimum((i + 1) * halo_blocks_per_tile, last_halo_block), z_block)),
            pl.BlockSpec((tm, D_MODEL), lambda i: (i, 0)),
            const((1, A_WIDTH)),
            const((A_HEADS, CHUNK, CHUNK)),
            const((A_HEADS, CHUNK, 1)),
            const((A_WIDTH, D_MODEL)),
            const((POOL_GROUPS, tm, tm + 2 * HALO)),
            const((POOL_GROUPS, POOL_GROUP_DIM, POOL_GROUP_DIM)),
            const((1, B_WIDTH)),
            const((B_WIDTH, D_MODEL)),
            const((D_MODEL, D_MODEL)),
        ],
        out_specs=pl.BlockSpec((tm, D_MODEL), lambda i: (i, 0)),
        out_shape=jax.ShapeDtypeStruct((t, D_MODEL), _F32),
        scratch_shapes=[pltpu.VMEM((tm, A_WIDTH), _BF16), pltpu.VMEM((tm, B_WIDTH), _BF16)],
        compiler_params=pltpu.CompilerParams(
            dimension_semantics=("arbitrary",),
            vmem_limit_bytes=VMEM_LIMIT_BYTES),
        name="mixer",
    )(proj, proj, proj, h, v_gain, w_spatial, b_spatial, w_proj_a, _pool_band_matrices(tm),
      w_pool, pool_scale, w_proj_b, w_out)


def kernel(x_prompt, x_sample, ffn1_norm, ffn1_w1, ffn1_w3, ffn1_w2, mix_norm, w_in, v_norm,
           w_spatial, b_spatial, w_proj_a, w_pool, pool_scale, w_proj_b, w_out,
           ffn2_norm, ffn2_w1, ffn2_w3, ffn2_w2, final_norm):
    depth = ffn1_w1.shape[0]
    assert depth >= 1, "the output norm is fused into the last layer's second FFN"
    bf = lambda w: w.astype(_BF16)
    rows = lambda v: v.reshape(v.shape[0], 1, -1)
    final_gain = final_norm.reshape(1, -1)
    ffn1 = (rows(ffn1_norm), bf(ffn1_w1), bf(ffn1_w3), bf(ffn1_w2))
    ffn2 = (rows(ffn2_norm), bf(ffn2_w1), bf(ffn2_w3), bf(ffn2_w2))
    proj_p = (rows(mix_norm), bf(w_in))
    mix_p = (rows(v_norm), bf(w_spatial), b_spatial[..., None], bf(w_proj_a), bf(w_pool),
             rows(pool_scale), bf(w_proj_b), bf(w_out))

    outs = []
    for x in (x_prompt, x_sample):
        batch, seq_len, _ = x.shape
        h = x.reshape(batch * seq_len, D_MODEL)
        for l in range(depth):
            h = _ffn(h, *(p[l] for p in ffn1), final_gain, final_norm=False)
            proj = _proj(h, *(p[l] for p in proj_p))
            h = _mixer(proj, h, *(p[l] for p in mix_p), seq_len=seq_len)
            h = _ffn(h, *(p[l] for p in ffn2), final_gain, final_norm=(l == depth - 1))
        outs.append(h.reshape(batch, seq_len, D_MODEL))
    return tuple(outs)
```

```python
import functools

import jax
import jax.numpy as jnp
import numpy as np
from jax import lax
from jax.experimental import pallas as pl
from jax.experimental.pallas import tpu as pltpu

EPS = 1e-6
D_MODEL = 2048
D_FF = 5632
CHUNK = 128
A_HEADS = 8
A_WIDTH = D_MODEL // 2
A_HEAD_DIM = A_WIDTH // A_HEADS
POOL_WINDOWS = (2, 4, 8, 16)
POOL_GROUPS = len(POOL_WINDOWS)
B_WIDTH = D_MODEL // 2
POOL_GROUP_DIM = B_WIDTH // POOL_GROUPS
IN_WIDTH = 2 * A_WIDTH + B_WIDTH + 2 * D_MODEL
Z_OFF = 2 * A_WIDTH
GA_OFF = Z_OFF + B_WIDTH
GB_OFF = GA_OFF + D_MODEL

BF16_SUBLANES = 16
HALO = BF16_SUBLANES
VMEM_LIMIT_BYTES = 60 * 1024 * 1024

FFN_TM = 1024
FFN_TF = 512
FFN_BLOCKS_PER_STEP = 2
PROJ_TM = 1024
PROJ_TN = 1792
MIX_TM = 256

_F32 = jnp.float32
_BF16 = jnp.bfloat16


def _rms_norm(x, gain):
    return x * lax.rsqrt(jnp.mean(x * x, axis=-1, keepdims=True) + EPS) * gain


def _sigmoid(x):
    return 0.5 * jnp.tanh(0.5 * x) + 0.5


def _dot(a, b):
    return jnp.dot(a, b, preferred_element_type=_F32)


def _cast_block_rows(rows, steps):
    for block in range(BF16_SUBLANES, rows + 1, BF16_SUBLANES):
        if rows % block == 0 and rows // block <= steps:
            return block
    raise ValueError(f"no bf16-tile-aligned row block of {rows} rows fits {steps} steps")


def _cast_specs(cast, grid):
    specs = []
    for w in cast:
        block = _cast_block_rows(w.shape[0], grid[0] * grid[1])
        last = w.shape[0] // block - 1
        specs.append(pl.BlockSpec(
            (block, w.shape[1]),
            lambda i, j, last=last: (jnp.minimum(i * grid[1] + j, last), 0)))
    return specs


def _cast_blocks(cast_in, cast_out):
    for src, dst in zip(cast_in, cast_out, strict=True):
        dst[...] = src[...].astype(_BF16)


def _ffn_kernel(x_hbm, g_ref, w1_ref, w3_ref, w2_ref, go_ref, *refs, tm, tf, n_blocks, n_cast,
                final_norm):
    cast_in, o_hbm, cast_out = refs[:n_cast], refs[n_cast], refs[n_cast + 1:2 * n_cast + 1]
    acc, n_scr, x_sem, o_sem = refs[2 * n_cast + 1:]
    i, s = pl.program_id(0), pl.program_id(1)
    n_tiles, last = pl.num_programs(0), pl.num_programs(1) - 1
    slot = i % 2
    acc_ref = acc.at[slot]

    def x_copy(tile, sl):
        return pltpu.make_async_copy(x_hbm.at[pl.ds(tile * tm, tm)], acc.at[sl], x_sem.at[sl])

    def o_copy(tile, sl):
        return pltpu.make_async_copy(acc.at[sl], o_hbm.at[pl.ds(tile * tm, tm)], o_sem.at[sl])

    @pl.when((i == 0) & (s == 0))
    def _():
        x_copy(0, 0).start()

    @pl.when(s == 0)
    def _():
        x_copy(i, slot).wait()

    @pl.when((s == 1) & (i >= 1))
    def _():
        o_copy(i - 1, 1 - slot).wait()

    @pl.when((s == 1) & (i + 1 < n_tiles))
    def _():
        x_copy(i + 1, 1 - slot).start()

    def update(blocks):
        _cast_blocks(cast_in, cast_out)
        n = n_scr[...]
        total = None
        for b in range(blocks):
            cols = slice(b * tf, (b + 1) * tf)
            h1 = _dot(n, w1_ref[:, cols])
            h3 = _dot(n, w3_ref[:, cols])
            half_h1 = 0.5 * h1
            gated = (half_h1 * (jnp.tanh(half_h1) + 1.0) * (0.5 * h3)).astype(_BF16)
            part = _dot(gated, w2_ref[cols, :])
            total = part if total is None else total + part
        return total

    blocks_last = n_blocks - (pl.cdiv(n_blocks, FFN_BLOCKS_PER_STEP) - 1) * FFN_BLOCKS_PER_STEP

    @pl.when(s == 0)
    def _():
        x = acc_ref[...]
        n_scr[...] = _rms_norm(x, g_ref[...]).astype(_BF16)
        acc_ref[...] = x + update(FFN_BLOCKS_PER_STEP)

    @pl.when((s != 0) & (s != last))
    def _():
        acc_ref[...] += update(FFN_BLOCKS_PER_STEP)

    @pl.when(s == last)
    def _():
        h = acc_ref[...] + update(blocks_last)
        acc_ref[...] = _rms_norm(h, go_ref[...]) if final_norm else h
        o_copy(i, slot).start()

    @pl.when((s == last) & (i == n_tiles - 1))
    def _():
        o_copy(i, slot).wait()


def _ffn(x, gain, w1, w3, w2, out_gain, *, final_norm, cast=()):
    t = x.shape[0]
    tm, tf = FFN_TM, FFN_TF
    n_blocks = D_FF // tf
    n_steps = pl.cdiv(n_blocks, FFN_BLOCKS_PER_STEP)
    assert n_steps >= 2, "the slot hand-over happens in step 1"
    step_cols = FFN_BLOCKS_PER_STEP * tf
    grid = (t // tm, n_steps)
    cast_specs = _cast_specs(cast, grid)
    outs = pl.pallas_call(
        functools.partial(_ffn_kernel, tm=tm, tf=tf, n_blocks=n_blocks, n_cast=len(cast),
                          final_norm=final_norm),
        grid=grid,
        in_specs=[
            pl.BlockSpec(memory_space=pl.ANY),
            pl.BlockSpec((1, D_MODEL), lambda i, s: (0, 0)),
            pl.BlockSpec((D_MODEL, step_cols), lambda i, s: (0, s)),
            pl.BlockSpec((D_MODEL, step_cols), lambda i, s: (0, s)),
            pl.BlockSpec((step_cols, D_MODEL), lambda i, s: (s, 0)),
            pl.BlockSpec((1, D_MODEL), lambda i, s: (0, 0)),
            *cast_specs,
        ],
        out_specs=[pl.BlockSpec(memory_space=pl.ANY), *cast_specs],
        out_shape=[jax.ShapeDtypeStruct((t, D_MODEL), _F32),
                   *(jax.ShapeDtypeStruct(w.shape, _BF16) for w in cast)],
        scratch_shapes=[pltpu.VMEM((2, tm, D_MODEL), _F32),
                        pltpu.VMEM((tm, D_MODEL), _BF16),
                        pltpu.SemaphoreType.DMA((2,)),
                        pltpu.SemaphoreType.DMA((2,))],
        compiler_params=pltpu.CompilerParams(
            dimension_semantics=("arbitrary", "arbitrary"),
            vmem_limit_bytes=VMEM_LIMIT_BYTES),
        name=("ffn_final" if final_norm else "ffn") + ("_cast" if cast else ""),
    )(x, gain, w1, w3, w2, out_gain, *cast)
    return outs[0], tuple(outs[1:])


def _proj_kernel(h_ref, g_ref, w_ref, *refs, n_cast):
    cast_in, o_ref, cast_out, n_scr = refs[:n_cast], refs[n_cast], refs[n_cast + 1:-1], refs[-1]

    def project():
        o_ref[...] = _dot(n_scr[...], w_ref[...]).astype(_BF16)
        _cast_blocks(cast_in, cast_out)

    @pl.when(pl.program_id(1) == 0)
    def _():
        n_scr[...] = _rms_norm(h_ref[...], g_ref[...]).astype(_BF16)
        project()

    @pl.when(pl.program_id(1) != 0)
    def _():
        project()


def _proj(h, gain, w_in, cast=()):
    t = h.shape[0]
    tm, tn = PROJ_TM, PROJ_TN
    grid = (t // tm, IN_WIDTH // tn)
    cast_specs = _cast_specs(cast, grid)
    outs = pl.pallas_call(
        functools.partial(_proj_kernel, n_cast=len(cast)),
        grid=grid,
        in_specs=[
            pl.BlockSpec((tm, D_MODEL), lambda i, j: (i, 0)),
            pl.BlockSpec((1, D_MODEL), lambda i, j: (0, 0)),
            pl.BlockSpec((D_MODEL, tn), lambda i, j: (0, j)),
            *cast_specs,
        ],
        out_specs=[pl.BlockSpec((tm, tn), lambda i, j: (i, j)), *cast_specs],
        out_shape=[jax.ShapeDtypeStruct((t, IN_WIDTH), _BF16),
                   *(jax.ShapeDtypeStruct(w.shape, _BF16) for w in cast)],
        scratch_shapes=[pltpu.VMEM((tm, D_MODEL), _BF16)],
        compiler_params=pltpu.CompilerParams(
            dimension_semantics=("arbitrary", "arbitrary"),
            vmem_limit_bytes=VMEM_LIMIT_BYTES),
        name="proj_cast" if cast else "proj",
    )(h, gain, w_in, *cast)
    return outs[0], tuple(outs[1:])


def _pool_band_matrices(tm):
    t = np.arange(tm)[:, None]
    j = np.arange(tm + 2 * HALO)[None, :] - HALO
    mats = [((j - t >= -(w // 2)) & (j - t <= w // 2 - 1)) for w in POOL_WINDOWS]
    return jnp.asarray(np.stack(mats).astype(np.float32), dtype=_BF16)


def _mixer_kernel(proj_ref, zprev_ref, znext_ref, h_ref, vg_ref, ws_ref, bs_ref, wpa_ref, band_ref,
                  wpool_ref, pscale_ref, wpb_ref, wout_ref, o_ref, a_scr, b_scr, *, tm, seq_len):
    tiles_per_seq = seq_len // tm
    tile_in_seq = pl.program_id(0) % tiles_per_seq

    z_tile = proj_ref[:, Z_OFF:Z_OFF + B_WIDTH]
    z_prev = jnp.where(tile_in_seq == 0, jnp.zeros_like(zprev_ref), zprev_ref[...])
    z_next = jnp.where(tile_in_seq == tiles_per_seq - 1, jnp.zeros_like(znext_ref), znext_ref[...])
    z_ext = jnp.concatenate([z_prev, z_tile, z_next], axis=0)
    gcols = [slice(g * POOL_GROUP_DIM, (g + 1) * POOL_GROUP_DIM) for g in range(POOL_GROUPS)]
    win_sums = [_dot(band_ref[g], z_ext[:, gcols[g]]) for g in range(POOL_GROUPS)]

    v = proj_ref[:, A_WIDTH:2 * A_WIDTH].astype(_F32)
    vn = _rms_norm(v, vg_ref[...]).astype(_BF16)
    chunks = [slice(c * CHUNK, (c + 1) * CHUNK) for c in range(tm // CHUNK)]
    for hd in range(A_HEADS):
        cols = slice(hd * A_HEAD_DIM, (hd + 1) * A_HEAD_DIM)
        v_chunks = jnp.concatenate([vn[rows, cols] for rows in chunks], axis=1)
        mixed = _dot(ws_ref[hd], v_chunks) + bs_ref[hd]
        for c, rows in enumerate(chunks):
            u = proj_ref[rows, cols].astype(_F32)
            a_scr[rows, cols] = (u * mixed[:, c * A_HEAD_DIM:(c + 1) * A_HEAD_DIM]).astype(_BF16)

    t = tile_in_seq * tm + lax.broadcasted_iota(jnp.int32, (tm, 1), 0)
    pooled = []
    for g, w in enumerate(POOL_WINDOWS):
        half = w // 2
        count = (jnp.minimum(t + half, seq_len) - jnp.maximum(t - half, 0)).astype(_F32)
        pooled.append((win_sums[g] / count - z_tile[:, gcols[g]].astype(_F32)).astype(_BF16))
    y_a = _dot(a_scr[...], wpa_ref[...])
    for g in range(POOL_GROUPS):
        mixed_b = _dot(pooled[g], wpool_ref[g]) * pscale_ref[:, gcols[g]]
        b_scr[:, gcols[g]] = mixed_b.astype(_BF16)
    y_b = _dot(b_scr[...], wpb_ref[...])

    ga = proj_ref[:, GA_OFF:GA_OFF + D_MODEL].astype(_F32)
    gb = proj_ref[:, GB_OFF:GB_OFF + D_MODEL].astype(_F32)
    merged = _sigmoid(ga) * y_a + _sigmoid(gb) * y_b
    o_ref[...] = h_ref[...] + _dot(merged.astype(_BF16), wout_ref[...])


def _mixer(proj, h, v_gain, w_spatial, b_spatial, w_proj_a, w_pool, pool_scale, w_proj_b, w_out, *,
           seq_len):
    t = h.shape[0]
    tm = MIX_TM
    halo_blocks_per_tile = tm // HALO
    last_halo_block = t // HALO - 1
    z_block = Z_OFF // B_WIDTH

    def const(shape):
        return pl.BlockSpec(shape, lambda i: (0,) * len(shape), pipeline_mode=pl.Buffered(1))

    return pl.pallas_call(
        functools.partial(_mixer_kernel, tm=tm, seq_len=seq_len),
        grid=(t // tm,),
        in_specs=[
            pl.BlockSpec((tm, IN_WIDTH), lambda i: (i, 0)),
            pl.BlockSpec((HALO, B_WIDTH),
                         lambda i: (jnp.maximum(i * halo_blocks_per_tile - 1, 0), z_block)),
            pl.BlockSpec((HALO, B_WIDTH),
                         lambda i: (jnp.minimum((i + 1) * halo_blocks_per_tile, last_halo_block), z_block)),
            pl.BlockSpec((tm, D_MODEL), lambda i: (i, 0)),
            const((1, A_WIDTH)),
            const((A_HEADS, CHUNK, CHUNK)),
            const((A_HEADS, CHUNK, 1)),
            const((A_WIDTH, D_MODEL)),
            const((POOL_GROUPS, tm, tm + 2 * HALO)),
            const((POOL_GROUPS, POOL_GROUP_DIM, POOL_GROUP_DIM)),
            const((1, B_WIDTH)),
            const((B_WIDTH, D_MODEL)),
            const((D_MODEL, D_MODEL)),
        ],
        out_specs=pl.BlockSpec((tm, D_MODEL), lambda i: (i, 0)),
        out_shape=jax.ShapeDtypeStruct((t, D_MODEL), _F32),
        scratch_shapes=[pltpu.VMEM((tm, A_WIDTH), _BF16), pltpu.VMEM((tm, B_WIDTH), _BF16)],
        compiler_params=pltpu.CompilerParams(
            dimension_semantics=("arbitrary",),
            vmem_limit_bytes=VMEM_LIMIT_BYTES),
        name="mixer",
    )(proj, proj, proj, h, v_gain, w_spatial, b_spatial, w_proj_a, _pool_band_matrices(tm),
      w_pool, pool_scale, w_proj_b, w_out)


def kernel(x_prompt, x_sample, ffn1_norm, ffn1_w1, ffn1_w3, ffn1_w2, mix_norm, w_in, v_norm,
           w_spatial, b_spatial, w_proj_a, w_pool, pool_scale, w_proj_b, w_out,
           ffn2_norm, ffn2_w1, ffn2_w3, ffn2_w2, final_norm):
    depth = ffn1_w1.shape[0]
    assert depth >= 1, "the output norm is fused into the last layer's second FFN"
    bf = lambda w: w.astype(_BF16)
    rows = lambda v: v.reshape(v.shape[0], 1, -1)
    final_gain = final_norm.reshape(1, -1)
    ffn1 = (rows(ffn1_norm), bf(ffn1_w1), bf(ffn1_w3), bf(ffn1_w2))
    gains = dict(mix=rows(mix_norm), v=rows(v_norm), pool=rows(pool_scale), ffn2=rows(ffn2_norm))
    as_2d = lambda w: w.reshape(-1, w.shape[-1])

    mix_bf16 = [None] * depth
    ffn2_bf16 = [None] * depth
    outs = []
    for x in (x_prompt, x_sample):
        batch, seq_len, _ = x.shape
        h = x.reshape(batch * seq_len, D_MODEL)
        for l in range(depth):
            if mix_bf16[l] is None:
                mix_f32 = (w_in[l], w_spatial[l], w_proj_a[l], w_pool[l], w_proj_b[l], w_out[l])
                h, copies = _ffn(h, *(p[l] for p in ffn1), final_gain, final_norm=False,
                                 cast=tuple(as_2d(w) for w in mix_f32))
                mix_bf16[l] = tuple(c.reshape(w.shape) for c, w in zip(copies, mix_f32))
                proj, ffn2_bf16[l] = _proj(h, gains["mix"][l], mix_bf16[l][0],
                                           cast=(ffn2_w1[l], ffn2_w3[l], ffn2_w2[l]))
            else:
                h, _ = _ffn(h, *(p[l] for p in ffn1), final_gain, final_norm=False)
                proj, _ = _proj(h, gains["mix"][l], mix_bf16[l][0])
            _, ws, wpa, wpool, wpb, wout = mix_bf16[l]
            h = _mixer(proj, h, gains["v"][l], ws, b_spatial[l][..., None], wpa, wpool,
                       gains["pool"][l], wpb, wout, seq_len=seq_len)
            h, _ = _ffn(h, gains["ffn2"][l], *ffn2_bf16[l], final_gain,
                        final_norm=(l == depth - 1))
        outs.append(h.reshape(batch, seq_len, D_MODEL))
    return tuple(outs)
```

```python
import functools

import jax
import jax.numpy as jnp
import numpy as np
from jax import lax
from jax.experimental import pallas as pl
from jax.experimental.pallas import tpu as pltpu

EPS = 1e-6
D_MODEL = 2048
D_FF = 5632
CHUNK = 128
A_HEADS = 8
A_WIDTH = D_MODEL // 2
A_HEAD_DIM = A_WIDTH // A_HEADS
POOL_WINDOWS = (2, 4, 8, 16)
POOL_GROUPS = len(POOL_WINDOWS)
B_WIDTH = D_MODEL // 2
POOL_GROUP_DIM = B_WIDTH // POOL_GROUPS
IN_WIDTH = 2 * A_WIDTH + B_WIDTH + 2 * D_MODEL
Z_OFF = 2 * A_WIDTH
GA_OFF = Z_OFF + B_WIDTH
GB_OFF = GA_OFF + D_MODEL

BF16_SUBLANES = 16
HALO = BF16_SUBLANES
VMEM_LIMIT_BYTES = 60 * 1024 * 1024

FFN_TM = 1024
FFN_TF = 512
FFN_BLOCKS_PER_STEP = 2
PROJ_TM = 1024
PROJ_TN = 1792
MIX_TM = 256
CAST_MAX_BLOCKS = 64

_F32 = jnp.float32
_BF16 = jnp.bfloat16


def _rms_norm(x, gain):
    return x * lax.rsqrt(jnp.mean(x * x, axis=-1, keepdims=True) + EPS) * gain


def _sigmoid(x):
    return 0.5 * jnp.tanh(0.5 * x) + 0.5


def _dot(a, b):
    return jnp.dot(a, b, preferred_element_type=_F32)


def _cast_block_rows(rows, steps):
    for block in range(BF16_SUBLANES, rows + 1, BF16_SUBLANES):
        if rows % block == 0 and rows // block <= steps:
            return block
    raise ValueError(f"no bf16-tile-aligned row block of {rows} rows fits {steps} steps")


def _cast_specs(cast, grid):
    specs = []
    for w in cast:
        block = _cast_block_rows(w.shape[0], min(grid[0] * grid[1], CAST_MAX_BLOCKS))
        last = w.shape[0] // block - 1
        specs.append(pl.BlockSpec(
            (block, w.shape[1]),
            lambda i, j, last=last: (jnp.minimum(i * grid[1] + j, last), 0)))
    return specs


def _cast_blocks(cast_in, cast_out):
    for src, dst in zip(cast_in, cast_out, strict=True):
        dst[...] = src[...].astype(_BF16)


def _ffn_kernel(x_hbm, *refs, tm, tf, n_blocks, n_cast, group_tiles, final_norm):
    n_groups = len(group_tiles)
    x_hbms = (x_hbm, *refs[:n_groups - 1])
    refs = refs[n_groups - 1:]
    g_ref, w1_ref, w3_ref, w2_ref, go_ref = refs[:5]
    refs = refs[5:]
    cast_in, o_hbms = refs[:n_cast], refs[n_cast:n_cast + n_groups]
    cast_out = refs[n_cast + n_groups:2 * n_cast + n_groups]
    acc, n_scr, x_sem, o_sem = refs[2 * n_cast + n_groups:]
    i, s = pl.program_id(0), pl.program_id(1)
    n_tiles, last = pl.num_programs(0), pl.num_programs(1) - 1
    slot = i % 2
    acc_ref = acc.at[slot]
    group_starts = [sum(group_tiles[:g]) for g in range(n_groups)]

    def x_copy(tile, sl, g=0):
        rows = pl.ds((tile - group_starts[g]) * tm, tm)
        return pltpu.make_async_copy(x_hbms[g].at[rows], acc.at[sl], x_sem.at[sl])

    def o_copy(tile, sl, g=0):
        rows = pl.ds((tile - group_starts[g]) * tm, tm)
        return pltpu.make_async_copy(acc.at[sl], o_hbms[g].at[rows], o_sem.at[sl])

    def start_in_group(copy, tile, sl):
        for g in range(n_groups):
            @pl.when((tile >= group_starts[g]) & (tile < group_starts[g] + group_tiles[g]))
            def _(g=g):
                copy(tile, sl, g).start()

    @pl.when((i == 0) & (s == 0))
    def _():
        x_copy(0, 0).start()

    @pl.when(s == 0)
    def _():
        x_copy(group_starts[0], slot).wait()

    @pl.when((s == 1) & (i >= 1))
    def _():
        o_copy(group_starts[0], 1 - slot).wait()

    @pl.when((s == 1) & (i + 1 < n_tiles))
    def _():
        start_in_group(x_copy, i + 1, 1 - slot)

    def update(blocks):
        _cast_blocks(cast_in, cast_out)
        n = n_scr[...]
        total = None
        for b in range(blocks):
            cols = slice(b * tf, (b + 1) * tf)
            h1 = _dot(n, w1_ref[:, cols])
            h3 = _dot(n, w3_ref[:, cols])
            half_h1 = 0.5 * h1
            gated = (half_h1 * (jnp.tanh(half_h1) + 1.0) * (0.5 * h3)).astype(_BF16)
            part = _dot(gated, w2_ref[cols, :])
            total = part if total is None else total + part
        return total

    blocks_last = n_blocks - (pl.cdiv(n_blocks, FFN_BLOCKS_PER_STEP) - 1) * FFN_BLOCKS_PER_STEP

    @pl.when(s == 0)
    def _():
        x = acc_ref[...]
        n_scr[...] = _rms_norm(x, g_ref[...]).astype(_BF16)
        acc_ref[...] = x + update(FFN_BLOCKS_PER_STEP)

    @pl.when((s != 0) & (s != last))
    def _():
        acc_ref[...] += update(FFN_BLOCKS_PER_STEP)

    @pl.when(s == last)
    def _():
        h = acc_ref[...] + update(blocks_last)
        acc_ref[...] = _rms_norm(h, go_ref[...]) if final_norm else h
        start_in_group(o_copy, i, slot)

    @pl.when((s == last) & (i == n_tiles - 1))
    def _():
        o_copy(group_starts[0], slot).wait()


def _ffn(xs, gain, w1, w3, w2, out_gain, *, final_norm, cast=()):
    tm, tf = FFN_TM, FFN_TF
    group_tiles = tuple(x.shape[0] // tm for x in xs)
    n_blocks = D_FF // tf
    n_steps = pl.cdiv(n_blocks, FFN_BLOCKS_PER_STEP)
    assert n_steps >= 2, "the slot hand-over happens in step 1"
    step_cols = FFN_BLOCKS_PER_STEP * tf
    grid = (sum(group_tiles), n_steps)
    cast_specs = _cast_specs(cast, grid)
    any_spec = pl.BlockSpec(memory_space=pl.ANY)
    outs = pl.pallas_call(
        functools.partial(_ffn_kernel, tm=tm, tf=tf, n_blocks=n_blocks, n_cast=len(cast),
                          group_tiles=group_tiles, final_norm=final_norm),
        grid=grid,
        in_specs=[
            *([any_spec] * len(xs)),
            pl.BlockSpec((1, D_MODEL), lambda i, s: (0, 0)),
            pl.BlockSpec((D_MODEL, step_cols), lambda i, s: (0, s)),
            pl.BlockSpec((D_MODEL, step_cols), lambda i, s: (0, s)),
            pl.BlockSpec((step_cols, D_MODEL), lambda i, s: (s, 0)),
            pl.BlockSpec((1, D_MODEL), lambda i, s: (0, 0)),
            *cast_specs,
        ],
        out_specs=[*([any_spec] * len(xs)), *cast_specs],
        out_shape=[*(jax.ShapeDtypeStruct(x.shape, _F32) for x in xs),
                   *(jax.ShapeDtypeStruct(w.shape, _BF16) for w in cast)],
        scratch_shapes=[pltpu.VMEM((2, tm, D_MODEL), _F32),
                        pltpu.VMEM((tm, D_MODEL), _BF16),
                        pltpu.SemaphoreType.DMA((2,)),
                        pltpu.SemaphoreType.DMA((2,))],
        compiler_params=pltpu.CompilerParams(
            dimension_semantics=("arbitrary", "arbitrary"),
            vmem_limit_bytes=VMEM_LIMIT_BYTES),
        name=("ffn_final" if final_norm else "ffn") + ("_cast" if cast else ""),
    )(*xs, gain, w1, w3, w2, out_gain, *cast)
    return list(outs[:len(xs)]), tuple(outs[len(xs):])


def _proj_kernel(h_ref, g_ref, w_ref, *refs, n_cast):
    cast_in, o_ref, cast_out, n_scr = refs[:n_cast], refs[n_cast], refs[n_cast + 1:-1], refs[-1]

    def project():
        o_ref[...] = _dot(n_scr[...], w_ref[...]).astype(_BF16)
        _cast_blocks(cast_in, cast_out)

    @pl.when(pl.program_id(1) == 0)
    def _():
        n_scr[...] = _rms_norm(h_ref[...], g_ref[...]).astype(_BF16)
        project()

    @pl.when(pl.program_id(1) != 0)
    def _():
        project()


def _proj(h, gain, w_in, cast=()):
    t = h.shape[0]
    tm, tn = PROJ_TM, PROJ_TN
    grid = (t // tm, IN_WIDTH // tn)
    cast_specs = _cast_specs(cast, grid)
    outs = pl.pallas_call(
        functools.partial(_proj_kernel, n_cast=len(cast)),
        grid=grid,
        in_specs=[
            pl.BlockSpec((tm, D_MODEL), lambda i, j: (i, 0)),
            pl.BlockSpec((1, D_MODEL), lambda i, j: (0, 0)),
            pl.BlockSpec((D_MODEL, tn), lambda i, j: (0, j)),
            *cast_specs,
        ],
        out_specs=[pl.BlockSpec((tm, tn), lambda i, j: (i, j)), *cast_specs],
        out_shape=[jax.ShapeDtypeStruct((t, IN_WIDTH), _BF16),
                   *(jax.ShapeDtypeStruct(w.shape, _BF16) for w in cast)],
        scratch_shapes=[pltpu.VMEM((tm, D_MODEL), _BF16)],
        compiler_params=pltpu.CompilerParams(
            dimension_semantics=("arbitrary", "arbitrary"),
            vmem_limit_bytes=VMEM_LIMIT_BYTES),
        name="proj_cast" if cast else "proj",
    )(h, gain, w_in, *cast)
    return outs[0], tuple(outs[1:])


def _pool_band_matrices(tm):
    t = np.arange(tm)[:, None]
    j = np.arange(tm + 2 * HALO)[None, :] - HALO
    mats = [((j - t >= -(w // 2)) & (j - t <= w // 2 - 1)) for w in POOL_WINDOWS]
    return jnp.asarray(np.stack(mats).astype(np.float32), dtype=_BF16)


def _mixer_kernel(proj_ref, zprev_ref, znext_ref, h_ref, vg_ref, ws_ref, bs_ref, wpa_ref, band_ref,
                  wpool_ref, pscale_ref, wpb_ref, wout_ref, o_ref, a_scr, b_scr, *, tm, seq_len):
    tiles_per_seq = seq_len // tm
    tile_in_seq = pl.program_id(0) % tiles_per_seq

    z_tile = proj_ref[:, Z_OFF:Z_OFF + B_WIDTH]
    z_prev = jnp.where(tile_in_seq == 0, jnp.zeros_like(zprev_ref), zprev_ref[...])
    z_next = jnp.where(tile_in_seq == tiles_per_seq - 1, jnp.zeros_like(znext_ref), znext_ref[...])
    z_ext = jnp.concatenate([z_prev, z_tile, z_next], axis=0)
    gcols = [slice(g * POOL_GROUP_DIM, (g + 1) * POOL_GROUP_DIM) for g in range(POOL_GROUPS)]
    win_sums = [_dot(band_ref[g], z_ext[:, gcols[g]]) for g in range(POOL_GROUPS)]

    v = proj_ref[:, A_WIDTH:2 * A_WIDTH].astype(_F32)
    vn = _rms_norm(v, vg_ref[...]).astype(_BF16)
    chunks = [slice(c * CHUNK, (c + 1) * CHUNK) for c in range(tm // CHUNK)]
    for hd in range(A_HEADS):
        cols = slice(hd * A_HEAD_DIM, (hd + 1) * A_HEAD_DIM)
        v_chunks = jnp.concatenate([vn[rows, cols] for rows in chunks], axis=1)
        mixed = _dot(ws_ref[hd], v_chunks) + bs_ref[hd]
        for c, rows in enumerate(chunks):
            u = proj_ref[rows, cols].astype(_F32)
            a_scr[rows, cols] = (u * mixed[:, c * A_HEAD_DIM:(c + 1) * A_HEAD_DIM]).astype(_BF16)

    t = tile_in_seq * tm + lax.broadcasted_iota(jnp.int32, (tm, 1), 0)
    pooled = []
    for g, w in enumerate(POOL_WINDOWS):
        half = w // 2
        count = (jnp.minimum(t + half, seq_len) - jnp.maximum(t - half, 0)).astype(_F32)
        pooled.append((win_sums[g] / count - z_tile[:, gcols[g]].astype(_F32)).astype(_BF16))
    y_a = _dot(a_scr[...], wpa_ref[...])
    for g in range(POOL_GROUPS):
        mixed_b = _dot(pooled[g], wpool_ref[g]) * pscale_ref[:, gcols[g]]
        b_scr[:, gcols[g]] = mixed_b.astype(_BF16)
    y_b = _dot(b_scr[...], wpb_ref[...])

    ga = proj_ref[:, GA_OFF:GA_OFF + D_MODEL].astype(_F32)
    gb = proj_ref[:, GB_OFF:GB_OFF + D_MODEL].astype(_F32)
    merged = _sigmoid(ga) * y_a + _sigmoid(gb) * y_b
    o_ref[...] = h_ref[...] + _dot(merged.astype(_BF16), wout_ref[...])


def _mixer(proj, h, v_gain, w_spatial, b_spatial, w_proj_a, w_pool, pool_scale, w_proj_b, w_out, *,
           seq_len):
    t = h.shape[0]
    tm = MIX_TM
    halo_blocks_per_tile = tm // HALO
    last_halo_block = t // HALO - 1
    z_block = Z_OFF // B_WIDTH

    def const(shape):
        return pl.BlockSpec(shape, lambda i: (0,) * len(shape), pipeline_mode=pl.Buffered(1))

    return pl.pallas_call(
        functools.partial(_mixer_kernel, tm=tm, seq_len=seq_len),
        grid=(t // tm,),
        in_specs=[
            pl.BlockSpec((tm, IN_WIDTH), lambda i: (i, 0)),
            pl.BlockSpec((HALO, B_WIDTH),
                         lambda i: (jnp.maximum(i * halo_blocks_per_tile - 1, 0), z_block)),
            pl.BlockSpec((HALO, B_WIDTH),
                         lambda i: (jnp.minimum((i + 1) * halo_blocks_per_tile, last_halo_block), z_block)),
            pl.BlockSpec((tm, D_MODEL), lambda i: (i, 0)),
            const((1, A_WIDTH)),
            const((A_HEADS, CHUNK, CHUNK)),
            const((A_HEADS, CHUNK, 1)),
            const((A_WIDTH, D_MODEL)),
            const((POOL_GROUPS, tm, tm + 2 * HALO)),
            const((POOL_GROUPS, POOL_GROUP_DIM, POOL_GROUP_DIM)),
            const((1, B_WIDTH)),
            const((B_WIDTH, D_MODEL)),
            const((D_MODEL, D_MODEL)),
        ],
        out_specs=pl.BlockSpec((tm, D_MODEL), lambda i: (i, 0)),
        out_shape=jax.ShapeDtypeStruct((t, D_MODEL), _F32),
        scratch_shapes=[pltpu.VMEM((tm, A_WIDTH), _BF16), pltpu.VMEM((tm, B_WIDTH), _BF16)],
        compiler_params=pltpu.CompilerParams(
            dimension_semantics=("arbitrary",),
            vmem_limit_bytes=VMEM_LIMIT_BYTES),
        name="mixer",
    )(proj, proj, proj, h, v_gain, w_spatial, b_spatial, w_proj_a, _pool_band_matrices(tm),
      w_pool, pool_scale, w_proj_b, w_out)


def kernel(x_prompt, x_sample, ffn1_norm, ffn1_w1, ffn1_w3, ffn1_w2, mix_norm, w_in, v_norm,
           w_spatial, b_spatial, w_proj_a, w_pool, pool_scale, w_proj_b, w_out,
           ffn2_norm, ffn2_w1, ffn2_w3, ffn2_w2, final_norm):
    depth = ffn1_w1.shape[0]
    assert depth >= 1, "the output norm is fused into the last layer's second FFN"
    bf = lambda w: w.astype(_BF16)
    rows = lambda v: v.reshape(v.shape[0], 1, -1)
    final_gain = final_norm.reshape(1, -1)
    ffn1 = (rows(ffn1_norm), bf(ffn1_w1), bf(ffn1_w3), bf(ffn1_w2))
    gains = dict(mix=rows(mix_norm), v=rows(v_norm), pool=rows(pool_scale), ffn2=rows(ffn2_norm))
    as_2d = lambda w: w.reshape(-1, w.shape[-1])

    groups = (x_prompt, x_sample)
    seq_lens = [x.shape[1] for x in groups]
    hs = [x.reshape(-1, D_MODEL) for x in groups]
    for l in range(depth):
        mix_f32 = (w_in[l], w_spatial[l], w_proj_a[l], w_pool[l], w_proj_b[l], w_out[l])
        hs, copies = _ffn(hs, *(p[l] for p in ffn1), final_gain, final_norm=False,
                          cast=tuple(as_2d(w) for w in mix_f32))
        w_in_bf, ws, wpa, wpool, wpb, wout = (c.reshape(w.shape) for c, w in zip(copies, mix_f32))
        ffn2_bf16 = None
        for g, (h, seq_len) in enumerate(zip(hs, seq_lens)):
            if ffn2_bf16 is None:
                proj, ffn2_bf16 = _proj(h, gains["mix"][l], w_in_bf,
                                        cast=(ffn2_w1[l], ffn2_w3[l], ffn2_w2[l]))
            else:
                proj, _ = _proj(h, gains["mix"][l], w_in_bf)
            hs[g] = _mixer(proj, h, gains["v"][l], ws, b_spatial[l][..., None], wpa, wpool,
                           gains["pool"][l], wpb, wout, seq_len=seq_len)
        hs, _ = _ffn(hs, gains["ffn2"][l], *ffn2_bf16, final_gain, final_norm=(l == depth - 1))
    return tuple(h.reshape(x.shape) for h, x in zip(hs, groups))
```
